```python
import math
import jax, jax.numpy as jnp
from jax import lax
import numpy as np

D_MODEL = 1024
BATCH = 16
SEQ = 4096
DEPTH = 4

N_MIXERS = 3
NORM_EPS = 1e-6
CONV_K = 4

MAMBA_D_INNER = 2 * D_MODEL
MAMBA_HEADDIM = 64
MAMBA_HEADS = MAMBA_D_INNER // MAMBA_HEADDIM
MAMBA_GROUPS = 8
MAMBA_D_STATE = 128
MAMBA_CHUNK = 256
MAMBA_CONV_CH = MAMBA_D_INNER + 2 * MAMBA_GROUPS * MAMBA_D_STATE
MAMBA_IN = MAMBA_D_INNER + MAMBA_CONV_CH + MAMBA_HEADS

S5_GROUP_SIZE = 16
S5_GROUPS = D_MODEL // S5_GROUP_SIZE
S5_STATE = 64
S5_CHUNK = 256

GDN_HEADS = D_MODEL // 128
GDN_DK = 128
GDN_DV = 256
GDN_CHUNK = 64
GDN_CONV_CH = GDN_HEADS * (2 * GDN_DK + GDN_DV)
GDN_IN = GDN_CONV_CH + GDN_HEADS * GDN_DV + 2 * GDN_HEADS

FFN_HIDDEN = (8 * D_MODEL + 3 * 256 - 1) // (3 * 256) * 256

N_MAMBA = (DEPTH + 2) // 3
N_S5 = (DEPTH + 1) // 3
N_GDN = DEPTH // 3

kernel_name = 'hybrid_mamba2_s5_gdn_trunk'


def rmsnorm(x, w):
    xf = x.astype(jnp.float32)
    y = xf * lax.rsqrt(jnp.mean(xf * xf, axis=-1, keepdims=True) + NORM_EPS)
    return (y * w.astype(jnp.float32)).astype(x.dtype)


def causal_dwconv(x, w, b=None):
    k, c = w.shape
    out = lax.conv_general_dilated(x, w[:, None, :].astype(x.dtype), window_strides=(1,),
                                   padding=[(k - 1, 0)], dimension_numbers=('NWC', 'WIO', 'NWC'),
                                   feature_group_count=c)
    if b is not None:
        out = out + b.astype(x.dtype)
    return out


def to_chunks(t, chunk):
    b, l = t.shape[:2]
    return jnp.moveaxis(t.reshape(b, l // chunk, chunk, *t.shape[2:]), 1, 0)


def from_chunks(t):
    t = jnp.moveaxis(t, 0, 1)
    return t.reshape(t.shape[0], t.shape[1] * t.shape[2], *t.shape[3:])


def l2norm(t):
    return t * lax.rsqrt(jnp.sum(t * t, axis=-1, keepdims=True) + 1e-6)


def ssd_scan(x, da, bm, cm):
    bsz, l, h, p = x.shape
    g, n = bm.shape[2:]
    r = h // g
    chunk = math.gcd(l, MAMBA_CHUNK)
    xc = to_chunks(x.reshape(bsz, l, g, r, p), chunk)
    ac = to_chunks(da.reshape(bsz, l, g, r), chunk)
    bc = to_chunks(bm, chunk)
    cc = to_chunks(cm, chunk)
    causal = jnp.tril(jnp.ones((chunk, chunk), dtype=bool))[None, :, :, None, None]

    def step(state, inp):
        xk, ak, bk, ck = inp
        a_cs = jnp.cumsum(ak, axis=1)
        seg = a_cs[:, :, None] - a_cs[:, None, :]
        decay = jnp.exp(jnp.where(causal, seg, -jnp.inf))
        cb = jnp.einsum('blgn,bsgn->blsg', ck, bk)
        y_diag = jnp.einsum('blsgr,bsgrp->blgrp', cb[..., None] * decay, xk)
        y_off = jnp.einsum('blgn,bgrpn->blgrp', ck, state) * jnp.exp(a_cs)[..., None]
        to_end = jnp.exp(a_cs[:, -1:] - a_cs)
        new_state = (state * jnp.exp(a_cs[:, -1])[..., None, None]
                     + jnp.einsum('bsgn,bsgrp->bgrpn', bk, xk * to_end[..., None]))
        return new_state, y_diag + y_off

    state0 = jnp.zeros((bsz, g, r, p, n), jnp.float32)
    _, ys = lax.scan(step, state0, (xc, ac, bc, cc))
    return from_chunks(ys).reshape(bsz, l, h, p)


def mamba2_mixer(h, w_in, conv_w, conv_b, dt_bias, a_log, d_skip, norm_w, w_out):
    bsz, l, _ = h.shape
    f32 = jnp.float32
    proj = h @ w_in
    z = proj[..., :MAMBA_D_INNER]
    xbc = proj[..., MAMBA_D_INNER:MAMBA_D_INNER + MAMBA_CONV_CH]
    dt = proj[..., MAMBA_D_INNER + MAMBA_CONV_CH:]
    xbc = jax.nn.silu(causal_dwconv(xbc, conv_w, conv_b)).astype(f32)
    gn = MAMBA_GROUPS * MAMBA_D_STATE
    xs = xbc[..., :MAMBA_D_INNER].reshape(bsz, l, MAMBA_HEADS, MAMBA_HEADDIM)
    bm = xbc[..., MAMBA_D_INNER:MAMBA_D_INNER + gn].reshape(bsz, l, MAMBA_GROUPS, MAMBA_D_STATE)
    cm = xbc[..., MAMBA_D_INNER + gn:].reshape(bsz, l, MAMBA_GROUPS, MAMBA_D_STATE)
    dt = jax.nn.softplus(dt.astype(f32) + dt_bias.astype(f32))
    a = -jnp.exp(a_log.astype(f32))
    y = ssd_scan(xs * dt[..., None], dt * a, bm, cm) + d_skip.astype(f32)[:, None] * xs
    gs = MAMBA_D_INNER // MAMBA_GROUPS
    y = y.reshape(bsz, l, MAMBA_GROUPS, gs) * jax.nn.silu(z.astype(f32)).reshape(bsz, l, MAMBA_GROUPS, gs)
    y = y * lax.rsqrt(jnp.mean(y * y, axis=-1, keepdims=True) + NORM_EPS)
    y = y.reshape(bsz, l, MAMBA_D_INNER) * norm_w.astype(f32)
    return y.astype(h.dtype) @ w_out


def s5_mixer(h, lam_re, lam_im, log_dt, b_re, b_im, c_re, c_im, d_skip, w_glu, b_glu):
    bsz, l, d = h.shape
    f32 = jnp.float32
    lr, li = lam_re.astype(f32), lam_im.astype(f32)
    b_re, b_im = b_re.astype(f32), b_im.astype(f32)
    c_re, c_im = c_re.astype(f32), c_im.astype(f32)
    dt = jnp.exp(log_dt.astype(f32))[:, None]
    mag = jnp.exp(lr * dt)
    lbar_re, lbar_im = mag * jnp.cos(li * dt), mag * jnp.sin(li * dt)
    den = lr * lr + li * li
    zr = ((lbar_re - 1.0) * lr + lbar_im * li) / den
    zi = (lbar_im * lr - (lbar_re - 1.0) * li) / den
    bb_re = zr[..., None] * b_re - zi[..., None] * b_im
    bb_im = zr[..., None] * b_im + zi[..., None] * b_re
    chunk = math.gcd(l, S5_CHUNK)
    u = h.astype(f32).reshape(bsz, l, S5_GROUPS, S5_GROUP_SIZE)
    uc = to_chunks(u, chunk)

    def combine(e1, e2):
        a1r, a1i, b1r, b1i = e1
        a2r, a2i, b2r, b2i = e2
        return (a2r * a1r - a2i * a1i, a2r * a1i + a2i * a1r,
                a2r * b1r - a2i * b1i + b2r, a2r * b1i + a2i * b1r + b2i)

    def step(carry, uk):
        s0r, s0i = carry
        bur = jnp.einsum('bcgi,gpi->bcgp', uk, bb_re)
        bui = jnp.einsum('bcgi,gpi->bcgp', uk, bb_im)
        ar = jnp.broadcast_to(lbar_re, bur.shape)
        ai = jnp.broadcast_to(lbar_im, bur.shape)
        pr, pim, sr, si = lax.associative_scan(combine, (ar, ai, bur, bui), axis=1)
        sr, si = (sr + pr * s0r[:, None] - pim * s0i[:, None],
                  si + pr * s0i[:, None] + pim * s0r[:, None])
        y = jnp.einsum('bcgp,gip->bcgi', sr, c_re) - jnp.einsum('bcgp,gip->bcgi', si, c_im)
        return (sr[:, -1], si[:, -1]), y

    carry0 = (jnp.zeros((bsz, S5_GROUPS, S5_STATE), f32), jnp.zeros((bsz, S5_GROUPS, S5_STATE), f32))
    _, ys = lax.scan(step, carry0, uc)
    y = from_chunks(ys).reshape(bsz, l, d) + d_skip.astype(f32) * h.astype(f32)
    g = jax.nn.gelu(y).astype(h.dtype)
    gl = g @ w_glu + b_glu
    return gl[..., :d] * jax.nn.sigmoid(gl[..., d:])


def gated_delta_rule(q, k, v, g, beta):
    bsz, l, hh, dk = q.shape
    dv = v.shape[-1]
    chunk = math.gcd(l, GDN_CHUNK)

    def heads_first(t):
        return jnp.swapaxes(to_chunks(t, chunk), 2, 3)

    causal = jnp.tril(jnp.ones((chunk, chunk), dtype=bool))
    strict = jnp.tril(jnp.ones((chunk, chunk), dtype=bool), -1)
    eye = jnp.eye(chunk, dtype=jnp.float32)

    def step(s, inp):
        qc, kc, vc, gk, bk = inp
        gc = jnp.cumsum(gk, axis=-1)
        decay = jnp.exp(jnp.where(causal, gc[..., :, None] - gc[..., None, :], -jnp.inf))
        kb = kc * bk[..., None]
        m = jnp.where(strict, jnp.einsum('bhid,bhjd->bhij', kb, kc) * decay, 0.0)
        rhs = jnp.concatenate([vc * bk[..., None], kb * jnp.exp(gc)[..., None]], axis=-1)
        sol = lax.linalg.triangular_solve(eye + m, rhs, left_side=True, lower=True, unit_diagonal=True)
        u, w = sol[..., :dv], sol[..., dv:]
        v_new = u - jnp.einsum('bhck,bhkv->bhcv', w, s)
        attn = jnp.einsum('bhik,bhjk->bhij', qc, kc) * decay
        o = (jnp.einsum('bhck,bhkv->bhcv', qc * jnp.exp(gc)[..., None], s)
             + jnp.einsum('bhij,bhjv->bhiv', attn, v_new))
        s = (s * jnp.exp(gc[..., -1])[..., None, None]
             + jnp.einsum('bhck,bhcv->bhkv', kc * jnp.exp(gc[..., -1:] - gc)[..., None], v_new))
        return s, o

    s0 = jnp.zeros((bsz, hh, dk, dv), jnp.float32)
    _, os_ = lax.scan(step, s0, (heads_first(q), heads_first(k), heads_first(v), heads_first(g), heads_first(beta)))
    return from_chunks(jnp.swapaxes(os_, 2, 3))


def gated_deltanet_mixer(h, w_in, conv_w, a_log, dt_bias, norm_w, w_out):
    bsz, l, _ = h.shape
    f32 = jnp.float32
    proj = h @ w_in
    qkv = jax.nn.silu(causal_dwconv(proj[..., :GDN_CONV_CH], conv_w)).astype(f32)
    kd = GDN_HEADS * GDN_DK
    q = l2norm(qkv[..., :kd].reshape(bsz, l, GDN_HEADS, GDN_DK)) * (GDN_DK ** -0.5)
    k = l2norm(qkv[..., kd:2 * kd].reshape(bsz, l, GDN_HEADS, GDN_DK))
    v = qkv[..., 2 * kd:].reshape(bsz, l, GDN_HEADS, GDN_DV)
    off = GDN_CONV_CH + GDN_HEADS * GDN_DV
    gate = proj[..., GDN_CONV_CH:off].astype(f32).reshape(bsz, l, GDN_HEADS, GDN_DV)
    beta = jax.nn.sigmoid(proj[..., off:off + GDN_HEADS].astype(f32))
    a = proj[..., off + GDN_HEADS:].astype(f32)
    g = -jnp.exp(a_log.astype(f32)) * jax.nn.softplus(a + dt_bias.astype(f32))
    o = gated_delta_rule(q, k, v, g, beta)
    o = o * lax.rsqrt(jnp.mean(o * o, axis=-1, keepdims=True) + NORM_EPS) * norm_w.astype(f32) * jax.nn.silu(gate)
    return o.reshape(bsz, l, GDN_HEADS * GDN_DV).astype(h.dtype) @ w_out


def swiglu(h, w_in, w_out):
    gu = h @ w_in
    return (jax.nn.silu(gu[..., :FFN_HIDDEN]) * gu[..., FFN_HIDDEN:]) @ w_out


def setup_inputs(seed: int = 0) -> dict:
    key = jax.random.key(seed)
    ks = iter(jax.random.split(key, 40))
    f32 = jnp.float32

    def nrm(shape, scale):
        return jax.random.normal(next(ks), shape, f32) * scale

    def unif(shape, lo, hi):
        return jax.random.uniform(next(ks), shape, f32, lo, hi)

    def dt_bias_init(shape):
        dt = jnp.exp(unif(shape, math.log(1e-3), math.log(1e-1)))
        return dt + jnp.log(-jnp.expm1(-dt))

    d = D_MODEL
    return {
        'x': nrm((BATCH, SEQ, d), 1.0),
        'mix_norm_w': 1.0 + nrm((DEPTH, d), 0.02),
        'mamba_w_in': nrm((N_MAMBA, d, MAMBA_IN), d ** -0.5),
        'mamba_conv_w': nrm((N_MAMBA, CONV_K, MAMBA_CONV_CH), CONV_K ** -0.5),
        'mamba_conv_b': nrm((N_MAMBA, MAMBA_CONV_CH), 0.02),
        'mamba_dt_bias': dt_bias_init((N_MAMBA, MAMBA_HEADS)),
        'mamba_a_log': jnp.log(unif((N_MAMBA, MAMBA_HEADS), 1.0, 16.0)),
        'mamba_d': 1.0 + nrm((N_MAMBA, MAMBA_HEADS), 0.02),
        'mamba_norm_w': 1.0 + nrm((N_MAMBA, MAMBA_D_INNER), 0.02),
        'mamba_w_out': nrm((N_MAMBA, MAMBA_D_INNER, d), MAMBA_D_INNER ** -0.5),
        's5_lam_re': -0.5 + nrm((N_S5, S5_GROUPS, S5_STATE), 0.01),
        's5_lam_im': jnp.broadcast_to(math.pi * jnp.arange(S5_STATE, dtype=f32), (N_S5, S5_GROUPS, S5_STATE)),
        's5_log_dt': unif((N_S5, S5_GROUPS), math.log(1e-3), math.log(1e-1)),
        's5_b_re': nrm((N_S5, S5_GROUPS, S5_STATE, S5_GROUP_SIZE), (2 * S5_GROUP_SIZE) ** -0.5),
        's5_b_im': nrm((N_S5, S5_GROUPS, S5_STATE, S5_GROUP_SIZE), (2 * S5_GROUP_SIZE) ** -0.5),
        's5_c_re': nrm((N_S5, S5_GROUPS, S5_GROUP_SIZE, S5_STATE), (2 * S5_STATE) ** -0.5),
        's5_c_im': nrm((N_S5, S5_GROUPS, S5_GROUP_SIZE, S5_STATE), (2 * S5_STATE) ** -0.5),
        's5_d': nrm((N_S5, d), 1.0),
        's5_w_glu': nrm((N_S5, d, 2 * d), d ** -0.5),
        's5_b_glu': nrm((N_S5, 2 * d), 0.02),
        'gdn_w_in': nrm((N_GDN, d, GDN_IN), d ** -0.5),
        'gdn_conv_w': nrm((N_GDN, CONV_K, GDN_CONV_CH), CONV_K ** -0.5),
        'gdn_a_log': jnp.log(unif((N_GDN, GDN_HEADS), 1.0, 16.0)),
        'gdn_dt_bias': dt_bias_init((N_GDN, GDN_HEADS)),
        'gdn_norm_w': 1.0 + nrm((N_GDN, GDN_DV), 0.02),
        'gdn_w_out': nrm((N_GDN, GDN_HEADS * GDN_DV, d), (GDN_HEADS * GDN_DV) ** -0.5),
        'ffn_norm_w': 1.0 + nrm((DEPTH, d), 0.02),
        'ffn_w_in': nrm((DEPTH, d, 2 * FFN_HIDDEN), d ** -0.5),
        'ffn_w_out': nrm((DEPTH, FFN_HIDDEN, d), FFN_HIDDEN ** -0.5),
        'final_norm_w': 1.0 + nrm((d,), 0.02),
    }


def reference(x, mix_norm_w, mamba_w_in, mamba_conv_w, mamba_conv_b, mamba_dt_bias, mamba_a_log, mamba_d,
              mamba_norm_w, mamba_w_out, s5_lam_re, s5_lam_im, s5_log_dt, s5_b_re, s5_b_im, s5_c_re, s5_c_im,
              s5_d, s5_w_glu, s5_b_glu, gdn_w_in, gdn_conv_w, gdn_a_log, gdn_dt_bias, gdn_norm_w, gdn_w_out,
              ffn_norm_w, ffn_w_in, ffn_w_out, final_norm_w):
    for i in range(DEPTH):
        kind = i % N_MIXERS
        j = i // N_MIXERS
        h = rmsnorm(x, mix_norm_w[i])
        if kind == 0:
            mix = mamba2_mixer(h, mamba_w_in[j], mamba_conv_w[j], mamba_conv_b[j], mamba_dt_bias[j],
                               mamba_a_log[j], mamba_d[j], mamba_norm_w[j], mamba_w_out[j])
        elif kind == 1:
            mix = s5_mixer(h, s5_lam_re[j], s5_lam_im[j], s5_log_dt[j], s5_b_re[j], s5_b_im[j],
                           s5_c_re[j], s5_c_im[j], s5_d[j], s5_w_glu[j], s5_b_glu[j])
        else:
            mix = gated_deltanet_mixer(h, gdn_w_in[j], gdn_conv_w[j], gdn_a_log[j], gdn_dt_bias[j],
                                       gdn_norm_w[j], gdn_w_out[j])
        x = x + mix.astype(x.dtype)
        x = x + swiglu(rmsnorm(x, ffn_norm_w[i]), ffn_w_in[i], ffn_w_out[i]).astype(x.dtype)
    return rmsnorm(x, final_norm_w)
```

```python
import functools
import math

import jax
import jax.numpy as jnp
from jax import lax
from jax.experimental import pallas as pl
from jax.experimental.pallas import tpu as pltpu

F32 = jnp.float32
BF16 = jnp.bfloat16
HIGHEST = lax.Precision.HIGHEST

NORM_EPS = 1e-6
CONV_K = 4
CONV_PAD = 8
LANES = 128
VMEM_LIMIT_BYTES = 56 * 1024 * 1024

MAMBA_HEADDIM = 64
MAMBA_GROUPS = 8
MAMBA_D_STATE = 128
MAMBA_CHUNK = 256

S5_GROUP_SIZE = 16
S5_STATE = 64
S5_TIME_TILE = 64
S5_LANE_BLOCKS = 8

GDN_DK = 128
GDN_DV = 256
GDN_CHUNK = 64
GDN_TIME_TILE = 128

FFN_ROW_TILE = 512


def _dot(a, b, **kw):
    return jnp.dot(a, b, preferred_element_type=F32, **kw)


def _dot_nt(a, b):
    return lax.dot_general(a, b, (((1,), (1,)), ((), ())), preferred_element_type=F32)


def _rms(x, w):
    return x * lax.rsqrt(jnp.mean(x * x, axis=-1, keepdims=True) + NORM_EPS) * w


def _softplus(x):
    return jnp.maximum(x, 0.0) + jnp.log1p(jnp.exp(-jnp.abs(x)))


def _silu(x):
    return x * jax.nn.sigmoid(x)


def _const_spec(shape):
    nd = len(shape)
    return pl.BlockSpec(shape, lambda *_: (0,) * nd, pipeline_mode=pl.Buffered(1))


def _stream_dims(x, d, time_major):
    if time_major:
        return x.shape[1] // d, x.shape[0]
    return x.shape[0], x.shape[1]


def _row_spec(tile, d, time_major):
    if time_major:
        return pl.BlockSpec((tile, d), lambda b, c: (c, b))
    return pl.BlockSpec((None, tile, d), lambda b, c: (b, c, 0))


def _stream_shape(bsz, l, d, time_major):
    return (l, bsz * d) if time_major else (bsz, l, d)


def _params(semantics):
    return pltpu.CompilerParams(dimension_semantics=semantics, vmem_limit_bytes=VMEM_LIMIT_BYTES)


def _ffn_kernel(x_ref, nw_ref, wg_ref, wu_ref, wo_ref, fw_ref, o_ref, *, hidden_chunk, final_norm):
    x = x_ref[...]
    hb = _rms(x, nw_ref[...]).astype(BF16)
    acc = x
    hidden = wg_ref.shape[1]
    for c0 in range(0, hidden, hidden_chunk):
        g = _dot(hb, wg_ref[:, c0:c0 + hidden_chunk])
        u = _dot(hb, wu_ref[:, c0:c0 + hidden_chunk])
        a = (_silu(g) * u).astype(BF16)
        acc = acc + _dot(a, wo_ref[c0:c0 + hidden_chunk, :])
    if final_norm:
        acc = _rms(acc, fw_ref[...])
    o_ref[...] = acc


def _ffn_layer(x, norm_w, w_in, w_out, final_w, *, in_tm, out_tm, final_norm):
    d = w_in.shape[0]
    bsz, l = _stream_dims(x, d, in_tm)
    hidden = w_out.shape[0]
    hidden_chunk = hidden // 2 if (hidden // 2) % LANES == 0 else hidden
    tile = min(FFN_ROW_TILE, l)
    wg = w_in[:, :hidden].astype(BF16)
    wu = w_in[:, hidden:].astype(BF16)
    wo = w_out.astype(BF16)
    kern = functools.partial(_ffn_kernel, hidden_chunk=hidden_chunk, final_norm=final_norm)
    return pl.pallas_call(
        kern,
        grid=(bsz, l // tile),
        in_specs=[_row_spec(tile, d, in_tm), _const_spec((1, d)), _const_spec((d, hidden)),
                  _const_spec((d, hidden)), _const_spec((hidden, d)), _const_spec((1, d))],
        out_specs=_row_spec(tile, d, out_tm),
        out_shape=jax.ShapeDtypeStruct(_stream_shape(bsz, l, d, out_tm), F32),
        compiler_params=_params(("parallel", "parallel")),
        name="swiglu_ffn",
    )(x, norm_w.reshape(1, d), wg, wu, wo, final_w.reshape(1, d))


def _pair_cols(mat, h0, lane_lt64):
    return jnp.where(lane_lt64, mat[:, h0:h0 + 1], mat[:, h0 + 1:h0 + 2])


def _mamba_kernel(x_ref, nw_ref, wz_ref, wxbc_ref, wdt_ref, cw_ref, cb_ref, dtb_ref, alog_ref,
                  dsk_ref, gnw_ref, wo_ref, o_ref, pad_ref, st_ref):
    q = x_ref.shape[0]
    d_inner = wz_ref.shape[1]
    n = MAMBA_D_STATE
    gs = d_inner // MAMBA_GROUPS
    heads_per_group = gs // MAMBA_HEADDIM

    @pl.when(pl.program_id(1) == 0)
    def _():
        pad_ref[0:CONV_PAD, :] = jnp.zeros((CONV_PAD, pad_ref.shape[1]), F32)
        st_ref[...] = jnp.zeros(st_ref.shape, F32)

    x = x_ref[...]
    hb = _rms(x, nw_ref[...]).astype(BF16)
    z = _dot(hb, wz_ref[...])
    xbc = _dot(hb, wxbc_ref[...])
    dtr = _dot(hb, wdt_ref[...])

    pad_ref[CONV_PAD:CONV_PAD + q, :] = xbc
    cw = cw_ref[...]
    conv = cb_ref[...] + cw[CONV_K - 1:CONV_K, :] * xbc
    for k in range(CONV_K - 1):
        s = CONV_K - 1 - k
        conv = conv + cw[k:k + 1, :] * pad_ref[CONV_PAD - s:CONV_PAD - s + q, :]
    pad_ref[0:CONV_PAD, :] = xbc[q - CONV_PAD:q, :]
    xbc = _silu(conv)
    xs = xbc[:, :d_inner]
    bm = xbc[:, d_inner:d_inner + MAMBA_GROUPS * n]
    cm = xbc[:, d_inner + MAMBA_GROUPS * n:]

    dt = _softplus(dtr + dtb_ref[...])
    da = dt * (-jnp.exp(alog_ref[...]))
    row = lax.broadcasted_iota(jnp.int32, (q, q), 0)
    col = lax.broadcasted_iota(jnp.int32, (q, q), 1)
    causal = row >= col
    acs = _dot(causal.astype(F32), da, precision=HIGHEST)
    acs_t = acs.T
    dt_t = dt.T
    eacs = jnp.exp(acs)
    alast = acs[q - 1:q, :]
    toend = jnp.exp(alast - acs) * dt
    elast = jnp.exp(alast)
    lane_lt64 = lax.broadcasted_iota(jnp.int32, (1, LANES), 1) < MAMBA_HEADDIM

    xsb = xs.astype(BF16)
    bmb = bm.astype(BF16)
    cmb = cm.astype(BF16)
    zero_pair = jnp.zeros((q, LANES), BF16)
    y_groups = []
    for g in range(MAMBA_GROUPS):
        bg = bmb[:, g * n:(g + 1) * n]
        cg = cmb[:, g * n:(g + 1) * n]
        cb = _dot_nt(cg, bg)
        state = st_ref[g]
        y_off = _dot(cg, state.astype(BF16))
        y_pairs, e_pairs, w_pairs, l_pairs = [], [], [], []
        for pr in range(heads_per_group // 2):
            h0 = g * heads_per_group + 2 * pr
            lane0 = h0 * MAMBA_HEADDIM
            x_pair = xsb[:, lane0:lane0 + LANES]
            lms = []
            for h in (h0, h0 + 1):
                seg = acs[:, h:h + 1] - acs_t[h:h + 1, :]
                decay = jnp.exp(jnp.where(causal, seg, -jnp.inf))
                lms.append((cb * decay * dt_t[h:h + 1, :]).astype(BF16))
            lhs = jnp.concatenate(lms, axis=1)
            rhs = jnp.concatenate([jnp.where(lane_lt64, x_pair, zero_pair),
                                   jnp.where(lane_lt64, zero_pair, x_pair)], axis=0)
            y_pairs.append(_dot(lhs, rhs))
            e_pairs.append(_pair_cols(eacs, h0, lane_lt64))
            w_pairs.append(_pair_cols(toend, h0, lane_lt64))
            l_pairs.append(_pair_cols(elast, h0, lane_lt64))
        y_diag = jnp.concatenate(y_pairs, axis=1)
        e_exp = jnp.concatenate(e_pairs, axis=1)
        w_exp = jnp.concatenate(w_pairs, axis=1)
        l_exp = jnp.concatenate(l_pairs, axis=1)
        xs_g = xs[:, g * gs:(g + 1) * gs]
        y_groups.append(y_diag + y_off * e_exp + dsk_ref[:, g * gs:(g + 1) * gs] * xs_g)
        bg_t = bm[:, g * n:(g + 1) * n].T.astype(BF16)
        st_ref[g] = state * l_exp + _dot(bg_t, (xs_g * w_exp).astype(BF16))

    outs = []
    for g in range(MAMBA_GROUPS):
        yg = y_groups[g] * _silu(z[:, g * gs:(g + 1) * gs])
        yg = yg * lax.rsqrt(jnp.mean(yg * yg, axis=-1, keepdims=True) + NORM_EPS)
        outs.append((yg * gnw_ref[:, g * gs:(g + 1) * gs]).astype(BF16))
    y = jnp.concatenate(outs, axis=1)
    o_ref[...] = x + _dot(y, wo_ref[...])


def _mamba_layer(x, norm_w, w_in, conv_w, conv_b, dt_bias, a_log, d_skip, gn_w, w_out, *, in_tm, out_tm):
    d = w_in.shape[0]
    bsz, l = _stream_dims(x, d, in_tm)
    d_inner = w_out.shape[0]
    heads = d_inner // MAMBA_HEADDIM
    conv_ch = d_inner + 2 * MAMBA_GROUPS * MAMBA_D_STATE
    gs = d_inner // MAMBA_GROUPS
    q = math.gcd(l, MAMBA_CHUNK)
    pad_h = LANES - heads
    wz = w_in[:, :d_inner].astype(BF16)
    wxbc = w_in[:, d_inner:d_inner + conv_ch].astype(BF16)
    wdt = jnp.pad(w_in[:, d_inner + conv_ch:], ((0, 0), (0, pad_h))).astype(BF16)
    dtb = jnp.pad(dt_bias, (0, pad_h)).reshape(1, LANES)
    alog = jnp.pad(a_log, (0, pad_h)).reshape(1, LANES)
    dsk = jnp.repeat(d_skip, MAMBA_HEADDIM).reshape(1, d_inner)
    return pl.pallas_call(
        _mamba_kernel,
        grid=(bsz, l // q),
        in_specs=[_row_spec(q, d, in_tm), _const_spec((1, d)), _const_spec((d, d_inner)),
                  _const_spec((d, conv_ch)), _const_spec((d, LANES)), _const_spec((CONV_K, conv_ch)),
                  _const_spec((1, conv_ch)), _const_spec((1, LANES)), _const_spec((1, LANES)),
                  _const_spec((1, d_inner)), _const_spec((1, d_inner)), _const_spec((d_inner, d))],
        out_specs=_row_spec(q, d, out_tm),
        out_shape=jax.ShapeDtypeStruct(_stream_shape(bsz, l, d, out_tm), F32),
        scratch_shapes=[pltpu.VMEM((CONV_PAD + q, conv_ch), F32),
                        pltpu.VMEM((MAMBA_GROUPS, MAMBA_D_STATE, gs), F32)],
        compiler_params=_params(("parallel", "arbitrary")),
        name="mamba2_mixer",
    )(x, norm_w.reshape(1, d), wz, wxbc, wdt, conv_w, conv_b.reshape(1, conv_ch), dtb, alog, dsk,
      gn_w.reshape(1, d_inner), w_out.astype(BF16))


def _s5_discretize_kernel(lr_ref, li_ref, logdt_ref, bre_ref, bim_ref, lbr_ref, lbi_ref, bbr_ref, bbi_ref):
    lr = lr_ref[...]
    li = li_ref[...]
    dt = jnp.exp(logdt_ref[...])
    mag = jnp.exp(lr * dt)
    lbr = mag * jnp.cos(li * dt)
    lbi = mag * jnp.sin(li * dt)
    den = lr * lr + li * li
    zr = ((lbr - 1.0) * lr + lbi * li) / den
    zi = (lbi * lr - (lbr - 1.0) * li) / den
    lbr_ref[...] = lbr
    lbi_ref[...] = lbi
    bbr_ref[...] = zr * bre_ref[...] - zi * bim_ref[...]
    bbi_ref[...] = zr * bim_ref[...] + zi * bre_ref[...]


def _s5_kernel(x_ref, nw_ref, bblk_ref, cblk_ref, lre_ref, lim_ref, dsk_ref, wglu_ref, bglu_ref, o_ref,
               hn_ref, bu_ref, y_ref, st_ref):
    tt, bsz, d = x_ref.shape
    rows = tt * bsz
    half = bu_ref.shape[2] // 2
    n_blocks = bblk_ref.shape[0]
    in_lanes = bblk_ref.shape[1]

    @pl.when(pl.program_id(0) == 0)
    def _():
        st_ref[...] = jnp.zeros(st_ref.shape, F32)

    x = x_ref[...].reshape(rows, d)
    hn_ref[...] = _rms(x, nw_ref[...])
    for j in range(n_blocks):
        hb = hn_ref[:, j * in_lanes:(j + 1) * in_lanes].astype(BF16)
        bu_ref[...] = _dot(hb, bblk_ref[j]).reshape(tt, bsz, 2 * half)
        ar = jnp.broadcast_to(lre_ref[j], (bsz, half))
        ai = jnp.broadcast_to(lim_ref[j], (bsz, half))
        st = st_ref[j]

        def step(t, carry):
            sr, si = carry
            v = bu_ref[t]
            nsr = ar * sr - ai * si + v[:, :half]
            nsi = ar * si + ai * sr + v[:, half:]
            bu_ref[t] = jnp.concatenate([nsr, nsi], axis=1)
            return nsr, nsi

        sr, si = lax.fori_loop(0, tt, step, (st[:, :half], st[:, half:]), unroll=4)
        st_ref[j] = jnp.concatenate([sr, si], axis=1)
        states = bu_ref[...].reshape(rows, 2 * half).astype(BF16)
        y_ref[:, j * in_lanes:(j + 1) * in_lanes] = _dot(states, cblk_ref[j])
    y = y_ref[...] + dsk_ref[...] * hn_ref[...]
    gb = jax.nn.gelu(y).astype(BF16)
    gl = _dot(gb, wglu_ref[...]) + bglu_ref[...]
    o_ref[...] = (x + gl[:, :d] * jax.nn.sigmoid(gl[:, d:])).reshape(tt, bsz, d)


def _s5_layer(x, norm_w, lam_re, lam_im, log_dt, b_re, b_im, c_re, c_im, d_skip, w_glu, b_glu):
    d = w_glu.shape[0]
    bsz, l = _stream_dims(x, d, True)
    groups, state = lam_re.shape
    gsz = b_re.shape[2]
    nb = S5_LANE_BLOCKS
    gpb = groups // nb
    tt = math.gcd(l, S5_TIME_TILE)

    def expand(a):
        return jnp.repeat(a, gsz, axis=1)

    flat = (groups, state * gsz)
    lbr_e, lbi_e, bbr, bbi = pl.pallas_call(
        _s5_discretize_kernel,
        out_shape=[jax.ShapeDtypeStruct(flat, F32)] * 4,
        name="s5_discretize",
    )(expand(lam_re), expand(lam_im), jnp.broadcast_to(log_dt[:, None], flat),
      b_re.reshape(flat), b_im.reshape(flat))
    lbar_re = lbr_e.reshape(groups, state, gsz)[:, :, 0]
    lbar_im = lbi_e.reshape(groups, state, gsz)[:, :, 0]
    eye = jnp.eye(gpb, dtype=F32)

    def in_block(bb):
        bb = bb.reshape(nb, gpb, state, gsz)
        return jnp.einsum('jgpi,gh->jgihp', bb, eye).reshape(nb, gpb * gsz, gpb * state)

    def out_block(c):
        c = c.reshape(nb, gpb, gsz, state)
        return jnp.einsum('jgip,gh->jhpgi', c, eye).reshape(nb, gpb * state, gpb * gsz)

    bblk = jnp.concatenate([in_block(bbr), in_block(bbi)], axis=2).astype(BF16)
    cblk = jnp.concatenate([out_block(c_re), out_block(-c_im)], axis=1).astype(BF16)
    half = gpb * state
    lre = lbar_re.reshape(nb, 1, half)
    lim = lbar_im.reshape(nb, 1, half)
    rows = tt * bsz
    in_lanes = gpb * gsz
    out = pl.pallas_call(
        _s5_kernel,
        grid=(l // tt,),
        in_specs=[pl.BlockSpec((tt, bsz, d), lambda c: (c, 0, 0)), _const_spec((1, d)),
                  _const_spec((nb, in_lanes, 2 * half)), _const_spec((nb, 2 * half, in_lanes)),
                  _const_spec((nb, 1, half)), _const_spec((nb, 1, half)), _const_spec((1, d)),
                  _const_spec((d, 2 * d)), _const_spec((1, 2 * d))],
        out_specs=pl.BlockSpec((tt, bsz, d), lambda c: (c, 0, 0)),
        out_shape=jax.ShapeDtypeStruct((l, bsz, d), F32),
        scratch_shapes=[pltpu.VMEM((rows, d), F32), pltpu.VMEM((tt, bsz, 2 * half), F32),
                        pltpu.VMEM((rows, d), F32), pltpu.VMEM((nb, bsz, 2 * half), F32)],
        compiler_params=_params(("arbitrary",)),
        name="s5_mixer",
    )(x.reshape(l, bsz, d), norm_w.reshape(1, d), bblk, cblk, lre, lim, d_skip.reshape(1, d),
      w_glu.astype(BF16), b_glu.reshape(1, 2 * d))
    return out.reshape(l, bsz * d)


def _unit_lower_inverse(ms, cs):
    nsys = ms.shape[0]
    tiles = cs // 8
    m_tiles = [ms[:, 8 * k:8 * k + 8, :] for k in range(tiles)]
    rows = lax.broadcasted_iota(jnp.int32, (nsys, 8, cs), 1)
    cols = lax.broadcasted_iota(jnp.int32, (nsys, 8, cs), 2)
    x_tiles = [(rows + 8 * k == cols).astype(F32) for k in range(tiles)]
    for j in range(cs - 1):
        kj, r = divmod(j, 8)
        row_j = x_tiles[kj][:, r:r + 1, :]
        for k in range(kj, tiles):
            x_tiles[k] = x_tiles[k] - m_tiles[k][:, :, j:j + 1] * row_j
    return jnp.concatenate(x_tiles, axis=1)


def _l2norm(t):
    return t * lax.rsqrt(jnp.sum(t * t, axis=-1, keepdims=True) + 1e-6)


def _gdn_kernel(x_ref, nw_ref, wqkv_ref, wgate_ref, wba_ref, cw_ref, alog_ref, dtb_ref, gnw_ref, wo_ref,
                o_ref, pad_ref, st_ref):
    tt = x_ref.shape[0]
    heads, dk, dv = st_ref.shape
    cs = GDN_CHUNK
    nsub = tt // cs
    kd = heads * dk

    @pl.when(pl.program_id(1) == 0)
    def _():
        pad_ref[0:CONV_PAD, :] = jnp.zeros((CONV_PAD, pad_ref.shape[1]), F32)
        st_ref[...] = jnp.zeros(st_ref.shape, F32)

    x = x_ref[...]
    hb = _rms(x, nw_ref[...]).astype(BF16)
    qkv = _dot(hb, wqkv_ref[...])
    gate = _dot(hb, wgate_ref[...])
    ba = _dot(hb, wba_ref[...])

    pad_ref[CONV_PAD:CONV_PAD + tt, :] = qkv
    cw = cw_ref[...]
    conv = cw[CONV_K - 1:CONV_K, :] * qkv
    for k in range(CONV_K - 1):
        s = CONV_K - 1 - k
        conv = conv + cw[k:k + 1, :] * pad_ref[CONV_PAD - s:CONV_PAD - s + tt, :]
    pad_ref[0:CONV_PAD, :] = qkv[tt - CONV_PAD:tt, :]
    qkv = _silu(conv)

    beta = jax.nn.sigmoid(ba)
    gdec = -jnp.exp(alog_ref[...]) * _softplus(ba + dtb_ref[...])
    row = lax.broadcasted_iota(jnp.int32, (tt, tt), 0)
    col = lax.broadcasted_iota(jnp.int32, (tt, tt), 1)
    shift = cs.bit_length() - 1
    same_chunk = lax.shift_right_logical(row, shift) == lax.shift_right_logical(col, shift)
    chunk_tril = jnp.logical_and(row >= col, same_chunk).astype(F32)
    gc = _dot(chunk_tril, gdec, precision=HIGHEST)
    egc = jnp.exp(gc)
    r64 = lax.broadcasted_iota(jnp.int32, (cs, cs), 0)
    c64 = lax.broadcasted_iota(jnp.int32, (cs, cs), 1)
    causal = r64 >= c64
    strict = r64 > c64
    gc_t, to_end, e_last = [], [], []
    for s in range(nsub):
        gcs = gc[s * cs:(s + 1) * cs, :]
        gc_t.append(gcs.T)
        to_end.append(jnp.exp(gcs[cs - 1:cs, :] - gcs))
        e_last.append(jnp.exp(gcs[cs - 1:cs, :]))

    qs_, ks_, kbs_, vbs_, decays, ms = {}, {}, {}, {}, {}, []
    for h in range(heads):
        qh = _l2norm(qkv[:, h * dk:(h + 1) * dk]) * (dk ** -0.5)
        kh = _l2norm(qkv[:, kd + h * dk:kd + (h + 1) * dk])
        vh = qkv[:, 2 * kd + h * dv:2 * kd + (h + 1) * dv]
        bcol = beta[:, h:h + 1]
        kbh = kh * bcol
        vbh = vh * bcol
        lane = heads + h
        for s in range(nsub):
            sl = slice(s * cs, (s + 1) * cs)
            seg = gc[sl, lane:lane + 1] - gc_t[s][lane:lane + 1, :]
            decay = jnp.exp(jnp.where(causal, seg, -jnp.inf))
            kk = _dot_nt(kbh[sl].astype(BF16), kh[sl].astype(BF16))
            ms.append(jnp.where(strict, kk * decay, 0.0))
            qs_[h, s], ks_[h, s], kbs_[h, s], vbs_[h, s], decays[h, s] = qh[sl], kh[sl], kbh[sl], vbh[sl], decay

    t_all = _unit_lower_inverse(jnp.stack(ms), cs)

    o_heads = []
    for h in range(heads):
        lane = heads + h
        state = st_ref[h]
        o_rows = []
        for s in range(nsub):
            sl = slice(s * cs, (s + 1) * cs)
            e_col = egc[sl, lane:lane + 1]
            q_s, k_s = qs_[h, s], ks_[h, s]
            rhs = jnp.concatenate([vbs_[h, s], kbs_[h, s] * e_col], axis=1).astype(BF16)
            sol = _dot(t_all[h * nsub + s].astype(BF16), rhs)
            wq = jnp.concatenate([sol[:, dv:], q_s * e_col], axis=0).astype(BF16)
            ws = _dot(wq, state.astype(BF16))
            v_new = sol[:, :dv] - ws[:cs]
            attn = _dot_nt(q_s.astype(BF16), k_s.astype(BF16)) * decays[h, s]
            v_new_b = v_new.astype(BF16)
            o_rows.append(ws[cs:] + _dot(attn.astype(BF16), v_new_b))
            k_dec = (k_s * to_end[s][:, lane:lane + 1]).T.astype(BF16)
            state = state * e_last[s][:, lane:lane + 1] + _dot(k_dec, v_new_b)
        st_ref[h] = state
        oh = jnp.concatenate(o_rows, axis=0)
        oh = oh * lax.rsqrt(jnp.mean(oh * oh, axis=-1, keepdims=True) + NORM_EPS)
        oh = oh * gnw_ref[...] * _silu(gate[:, h * dv:(h + 1) * dv])
        o_heads.append(oh.astype(BF16))
    o_ref[...] = x + _dot(jnp.concatenate(o_heads, axis=1), wo_ref[...])


def _gdn_layer(x, norm_w, w_in, conv_w, a_log, dt_bias, gn_w, w_out):
    d = w_in.shape[0]
    bsz, l = _stream_dims(x, d, True)
    dk, dv = GDN_DK, GDN_DV
    heads = w_out.shape[0] // dv
    conv_ch = heads * (2 * dk + dv)
    off = conv_ch + heads * dv
    tt = math.gcd(l, GDN_TIME_TILE)
    lane_pad = LANES - 2 * heads
    wqkv = w_in[:, :conv_ch].astype(BF16)
    wgate = w_in[:, conv_ch:off].astype(BF16)
    wba = jnp.pad(w_in[:, off:], ((0, 0), (0, lane_pad))).astype(BF16)
    alog = jnp.pad(a_log, (heads, lane_pad)).reshape(1, LANES)
    dtb = jnp.pad(dt_bias, (heads, lane_pad)).reshape(1, LANES)
    return pl.pallas_call(
        _gdn_kernel,
        grid=(bsz, l // tt),
        in_specs=[_row_spec(tt, d, True), _const_spec((1, d)), _const_spec((d, conv_ch)),
                  _const_spec((d, heads * dv)), _const_spec((d, LANES)), _const_spec((CONV_K, conv_ch)),
                  _const_spec((1, LANES)), _const_spec((1, LANES)), _const_spec((1, dv)),
                  _const_spec((heads * dv, d))],
        out_specs=_row_spec(tt, d, True),
        out_shape=jax.ShapeDtypeStruct(_stream_shape(bsz, l, d, True), F32),
        scratch_shapes=[pltpu.VMEM((CONV_PAD + tt, conv_ch), F32), pltpu.VMEM((heads, dk, dv), F32)],
        compiler_params=_params(("parallel", "arbitrary")),
        name="gdn_mixer",
    )(x, norm_w.reshape(1, d), wqkv, wgate, wba, conv_w, alog, dtb, gn_w.reshape(1, dv), w_out.astype(BF16))


def kernel(x, mix_norm_w, mamba_w_in, mamba_conv_w, mamba_conv_b, mamba_dt_bias, mamba_a_log, mamba_d, mamba_norm_w, mamba_w_out, s5_lam_re, s5_lam_im, s5_log_dt, s5_b_re, s5_b_im, s5_c_re, s5_c_im, s5_d, s5_w_glu, s5_b_glu, gdn_w_in, gdn_conv_w, gdn_a_log, gdn_dt_bias, gdn_norm_w, gdn_w_out, ffn_norm_w, ffn_w_in, ffn_w_out, final_norm_w):
    depth = mix_norm_w.shape[0]
    h = x
    for i in range(depth):
        kind, j = i % 3, i // 3
        if kind == 0:
            h = _mamba_layer(h, mix_norm_w[i], mamba_w_in[j], mamba_conv_w[j], mamba_conv_b[j], mamba_dt_bias[j],
                             mamba_a_log[j], mamba_d[j], mamba_norm_w[j], mamba_w_out[j],
                             in_tm=i > 0, out_tm=True)
        elif kind == 1:
            h = _s5_layer(h, mix_norm_w[i], s5_lam_re[j], s5_lam_im[j], s5_log_dt[j], s5_b_re[j], s5_b_im[j],
                          s5_c_re[j], s5_c_im[j], s5_d[j], s5_w_glu[j], s5_b_glu[j])
        else:
            h = _gdn_layer(h, mix_norm_w[i], gdn_w_in[j], gdn_conv_w[j], gdn_a_log[j], gdn_dt_bias[j],
                           gdn_norm_w[j], gdn_w_out[j])
        last = i == depth - 1
        h = _ffn_layer(h, ffn_norm_w[i], ffn_w_in[i], ffn_w_out[i], final_norm_w,
                       in_tm=True, out_tm=not last, final_norm=last)
    return h
```

```python
import functools
import math

import jax
import jax.numpy as jnp
from jax import lax
from jax.experimental import pallas as pl
from jax.experimental.pallas import tpu as pltpu

F32 = jnp.float32
BF16 = jnp.bfloat16
HIGHEST = lax.Precision.HIGHEST

NORM_EPS = 1e-6
LOG2E = 1.4426950408889634
CONV_K = 4
CONV_PAD = 8
CONV_BLOCK = 512
LANES = 128
VMEM_LIMIT_BYTES = 56 * 1024 * 1024

MAMBA_HEADDIM = 64
MAMBA_GROUPS = 8
MAMBA_D_STATE = 128
MAMBA_CHUNK = 256

S5_GROUP_SIZE = 16
S5_STATE = 64
S5_TIME_TILE = 64
S5_LANE_BLOCKS = 8

GDN_DK = 128
GDN_DV = 256
GDN_CHUNK = 64
GDN_INV_BLOCK = 16
GDN_TIME_TILE = 128

FFN_ROW_TILE = 512


def _dot(a, b, **kw):
    return jnp.dot(a, b, preferred_element_type=F32, **kw)


def _dot_nt(a, b):
    return lax.dot_general(a, b, (((1,), (1,)), ((), ())), preferred_element_type=F32)


def _rms(x, w):
    return x * lax.rsqrt(jnp.mean(x * x, axis=-1, keepdims=True) + NORM_EPS) * w


def _softplus(x):
    return jnp.maximum(x, 0.0) + jnp.log1p(jnp.exp(-jnp.abs(x)))


def _silu(x):
    h = 0.5 * x
    return h + h * jnp.tanh(h)


def _const_spec(shape):
    nd = len(shape)
    return pl.BlockSpec(shape, lambda *_: (0,) * nd, pipeline_mode=pl.Buffered(1))


def _stream_dims(x, d, time_major):
    if time_major:
        return x.shape[1] // d, x.shape[0]
    return x.shape[0], x.shape[1]


def _row_spec(tile, d, time_major):
    if time_major:
        return pl.BlockSpec((tile, d), lambda b, c: (c, b))
    return pl.BlockSpec((None, tile, d), lambda b, c: (b, c, 0))


def _stream_shape(bsz, l, d, time_major):
    return (l, bsz * d) if time_major else (bsz, l, d)


def _params(semantics):
    return pltpu.CompilerParams(dimension_semantics=semantics, vmem_limit_bytes=VMEM_LIMIT_BYTES)


def _ffn_kernel(x_ref, nw_ref, wg_ref, wu_ref, wo_ref, fw_ref, o_ref, *, hidden_chunk, final_norm):
    x = x_ref[...]
    hb = _rms(x, nw_ref[...]).astype(BF16)
    acc = x
    hidden = wg_ref.shape[1]
    for c0 in range(0, hidden, hidden_chunk):
        g = _dot(hb, wg_ref[:, c0:c0 + hidden_chunk])
        u = _dot(hb, wu_ref[:, c0:c0 + hidden_chunk])
        a = (_silu(g) * u).astype(BF16)
        acc = acc + _dot(a, wo_ref[c0:c0 + hidden_chunk, :])
    if final_norm:
        acc = _rms(acc, fw_ref[...])
    o_ref[...] = acc


def _ffn_layer(x, norm_w, w_in, w_out, final_w, *, in_tm, out_tm, final_norm):
    d = w_in.shape[0]
    bsz, l = _stream_dims(x, d, in_tm)
    hidden = w_out.shape[0]
    hidden_chunk = hidden // 2 if (hidden // 2) % LANES == 0 else hidden
    tile = min(FFN_ROW_TILE, l)
    wg = w_in[:, :hidden].astype(BF16)
    wu = w_in[:, hidden:].astype(BF16)
    wo = w_out.astype(BF16)
    kern = functools.partial(_ffn_kernel, hidden_chunk=hidden_chunk, final_norm=final_norm)
    return pl.pallas_call(
        kern,
        grid=(bsz, l // tile),
        in_specs=[_row_spec(tile, d, in_tm), _const_spec((1, d)), _const_spec((d, hidden)),
                  _const_spec((d, hidden)), _const_spec((hidden, d)), _const_spec((1, d))],
        out_specs=_row_spec(tile, d, out_tm),
        out_shape=jax.ShapeDtypeStruct(_stream_shape(bsz, l, d, out_tm), F32),
        compiler_params=_params(("parallel", "parallel")),
        name="swiglu_ffn",
    )(x, norm_w.reshape(1, d), wg, wu, wo, final_w.reshape(1, d))


def _proj_conv_silu(hb, w_ref, cw_ref, cb_ref, pad_ref):
    rows = hb.shape[0]
    parts = []
    for j in range(pad_ref.shape[0]):
        c0 = j * CONV_BLOCK
        xb = _dot(hb, w_ref[:, c0:c0 + CONV_BLOCK])
        pad_ref[j, CONV_PAD:CONV_PAD + rows, :] = xb
        conv = cw_ref[CONV_K - 1:CONV_K, c0:c0 + CONV_BLOCK] * xb
        if cb_ref is not None:
            conv = conv + cb_ref[:, c0:c0 + CONV_BLOCK]
        for k in range(CONV_K - 1):
            s = CONV_K - 1 - k
            conv = conv + cw_ref[k:k + 1, c0:c0 + CONV_BLOCK] * pad_ref[j, CONV_PAD - s:CONV_PAD - s + rows, :]
        pad_ref[j, 0:CONV_PAD, :] = xb[rows - CONV_PAD:rows, :]
        parts.append(_silu(conv))
    return jnp.concatenate(parts, axis=1)


def _pair_cols(mat, h0, lane_lt64):
    return jnp.where(lane_lt64, mat[:, h0:h0 + 1], mat[:, h0 + 1:h0 + 2])


def _mamba_kernel(x_ref, nw_ref, wz_ref, wxbc_ref, wdt_ref, cw_ref, cb_ref, dtb_ref, alog_ref,
                  dsk_ref, gnw_ref, wo_ref, o_ref, pad_ref, st_ref):
    q = x_ref.shape[0]
    d_inner = wz_ref.shape[1]
    n = MAMBA_D_STATE
    gs = d_inner // MAMBA_GROUPS
    heads_per_group = gs // MAMBA_HEADDIM

    @pl.when(pl.program_id(1) == 0)
    def _():
        pad_ref[:, 0:CONV_PAD, :] = jnp.zeros((pad_ref.shape[0], CONV_PAD, pad_ref.shape[2]), F32)
        st_ref[...] = jnp.zeros(st_ref.shape, F32)

    x = x_ref[...]
    hb = _rms(x, nw_ref[...]).astype(BF16)
    dtr = _dot(hb, wdt_ref[...])
    xbc = _proj_conv_silu(hb, wxbc_ref, cw_ref, cb_ref, pad_ref)
    xs = xbc[:, :d_inner]
    bm = xbc[:, d_inner:d_inner + MAMBA_GROUPS * n]
    cm = xbc[:, d_inner + MAMBA_GROUPS * n:]

    dt = _softplus(dtr + dtb_ref[...])
    da = dt * (-jnp.exp(alog_ref[...]))
    row = lax.broadcasted_iota(jnp.int32, (q, q), 0)
    col = lax.broadcasted_iota(jnp.int32, (q, q), 1)
    causal = row >= col
    acs = _dot(causal.astype(F32), da, precision=HIGHEST)
    acs2 = acs * LOG2E
    acs2_t = acs2.T
    dt_t = dt.T
    eacs = jnp.exp(acs)
    alast = acs[q - 1:q, :]
    toend = jnp.exp(alast - acs) * dt
    elast = jnp.exp(alast)
    lane_lt64 = lax.broadcasted_iota(jnp.int32, (1, LANES), 1) < MAMBA_HEADDIM

    xsb = xs.astype(BF16)
    bmb = bm.astype(BF16)
    cmb = cm.astype(BF16)
    zero_pair = jnp.zeros((q, LANES), BF16)
    y_groups = []
    for g in range(MAMBA_GROUPS):
        bg = bmb[:, g * n:(g + 1) * n]
        cg = cmb[:, g * n:(g + 1) * n]
        cb = jnp.where(causal, _dot_nt(cg, bg), 0.0)
        state = st_ref[g]
        y_off = _dot(cg, state.astype(BF16))
        y_pairs, e_pairs, w_pairs, l_pairs = [], [], [], []
        for pr in range(heads_per_group // 2):
            h0 = g * heads_per_group + 2 * pr
            lane0 = h0 * MAMBA_HEADDIM
            x_pair = xsb[:, lane0:lane0 + LANES]
            lms = []
            for h in (h0, h0 + 1):
                seg2 = acs2[:, h:h + 1] - acs2_t[h:h + 1, :]
                lms.append((cb * jnp.exp2(jnp.minimum(seg2, 0.0)) * dt_t[h:h + 1, :]).astype(BF16))
            lhs = jnp.concatenate(lms, axis=1)
            rhs = jnp.concatenate([jnp.where(lane_lt64, x_pair, zero_pair),
                                   jnp.where(lane_lt64, zero_pair, x_pair)], axis=0)
            y_pairs.append(_dot(lhs, rhs))
            e_pairs.append(_pair_cols(eacs, h0, lane_lt64))
            w_pairs.append(_pair_cols(toend, h0, lane_lt64))
            l_pairs.append(_pair_cols(elast, h0, lane_lt64))
        y_diag = jnp.concatenate(y_pairs, axis=1)
        e_exp = jnp.concatenate(e_pairs, axis=1)
        w_exp = jnp.concatenate(w_pairs, axis=1)
        l_exp = jnp.concatenate(l_pairs, axis=1)
        xs_g = xs[:, g * gs:(g + 1) * gs]
        y_groups.append(y_diag + y_off * e_exp + dsk_ref[:, g * gs:(g + 1) * gs] * xs_g)
        bg_t = bm[:, g * n:(g + 1) * n].T.astype(BF16)
        st_ref[g] = state * l_exp + _dot(bg_t, (xs_g * w_exp).astype(BF16))

    z = _dot(hb, wz_ref[...])
    outs = []
    for g in range(MAMBA_GROUPS):
        yg = y_groups[g] * _silu(z[:, g * gs:(g + 1) * gs])
        yg = yg * lax.rsqrt(jnp.mean(yg * yg, axis=-1, keepdims=True) + NORM_EPS)
        outs.append((yg * gnw_ref[:, g * gs:(g + 1) * gs]).astype(BF16))
    y = jnp.concatenate(outs, axis=1)
    o_ref[...] = x + _dot(y, wo_ref[...])


def _mamba_layer(x, norm_w, w_in, conv_w, conv_b, dt_bias, a_log, d_skip, gn_w, w_out, *, in_tm, out_tm):
    d = w_in.shape[0]
    bsz, l = _stream_dims(x, d, in_tm)
    d_inner = w_out.shape[0]
    heads = d_inner // MAMBA_HEADDIM
    conv_ch = d_inner + 2 * MAMBA_GROUPS * MAMBA_D_STATE
    gs = d_inner // MAMBA_GROUPS
    q = math.gcd(l, MAMBA_CHUNK)
    pad_h = LANES - heads
    wz = w_in[:, :d_inner].astype(BF16)
    wxbc = w_in[:, d_inner:d_inner + conv_ch].astype(BF16)
    wdt = jnp.pad(w_in[:, d_inner + conv_ch:], ((0, 0), (0, pad_h))).astype(BF16)
    dtb = jnp.pad(dt_bias, (0, pad_h)).reshape(1, LANES)
    alog = jnp.pad(a_log, (0, pad_h)).reshape(1, LANES)
    dsk = jnp.repeat(d_skip, MAMBA_HEADDIM).reshape(1, d_inner)
    return pl.pallas_call(
        _mamba_kernel,
        grid=(bsz, l // q),
        in_specs=[_row_spec(q, d, in_tm), _const_spec((1, d)), _const_spec((d, d_inner)),
                  _const_spec((d, conv_ch)), _const_spec((d, LANES)), _const_spec((CONV_K, conv_ch)),
                  _const_spec((1, conv_ch)), _const_spec((1, LANES)), _const_spec((1, LANES)),
                  _const_spec((1, d_inner)), _const_spec((1, d_inner)), _const_spec((d_inner, d))],
        out_specs=_row_spec(q, d, out_tm),
        out_shape=jax.ShapeDtypeStruct(_stream_shape(bsz, l, d, out_tm), F32),
        scratch_shapes=[pltpu.VMEM((conv_ch // CONV_BLOCK, CONV_PAD + q, CONV_BLOCK), F32),
                        pltpu.VMEM((MAMBA_GROUPS, MAMBA_D_STATE, gs), F32)],
        compiler_params=_params(("parallel", "arbitrary")),
        name="mamba2_mixer",
    )(x, norm_w.reshape(1, d), wz, wxbc, wdt, conv_w, conv_b.reshape(1, conv_ch), dtb, alog, dsk,
      gn_w.reshape(1, d_inner), w_out.astype(BF16))


def _s5_discretize_kernel(lr_ref, li_ref, logdt_ref, bre_ref, bim_ref, lbr_ref, lbi_ref, bbr_ref, bbi_ref):
    lr = lr_ref[...]
    li = li_ref[...]
    dt = jnp.exp(logdt_ref[...])
    mag = jnp.exp(lr * dt)
    lbr = mag * jnp.cos(li * dt)
    lbi = mag * jnp.sin(li * dt)
    den = lr * lr + li * li
    zr = ((lbr - 1.0) * lr + lbi * li) / den
    zi = (lbi * lr - (lbr - 1.0) * li) / den
    lbr_ref[...] = lbr
    lbi_ref[...] = lbi
    bbr_ref[...] = zr * bre_ref[...] - zi * bim_ref[...]
    bbi_ref[...] = zr * bim_ref[...] + zi * bre_ref[...]


def _s5_kernel(x_ref, nw_ref, bblk_ref, cblk_ref, lre_ref, lim_ref, dsk_ref, wglu_ref, bglu_ref, o_ref,
               hn_ref, bu_ref, y_ref, st_ref):
    tt, bsz, d = x_ref.shape
    rows = tt * bsz
    half = bu_ref.shape[2] // 2
    n_blocks = bblk_ref.shape[0]
    in_lanes = bblk_ref.shape[1]

    @pl.when(pl.program_id(0) == 0)
    def _():
        st_ref[...] = jnp.zeros(st_ref.shape, F32)

    x = x_ref[...].reshape(rows, d)
    hn_ref[...] = _rms(x, nw_ref[...])
    for j in range(n_blocks):
        hb = hn_ref[:, j * in_lanes:(j + 1) * in_lanes].astype(BF16)
        bu_ref[...] = _dot(hb, bblk_ref[j]).reshape(tt, bsz, 2 * half)
        ar = jnp.broadcast_to(lre_ref[j], (bsz, half))
        ai = jnp.broadcast_to(lim_ref[j], (bsz, half))
        st = st_ref[j]

        def step(t, carry):
            sr, si = carry
            v = bu_ref[t]
            nsr = ar * sr - ai * si + v[:, :half]
            nsi = ar * si + ai * sr + v[:, half:]
            bu_ref[t] = jnp.concatenate([nsr, nsi], axis=1)
            return nsr, nsi

        sr, si = lax.fori_loop(0, tt, step, (st[:, :half], st[:, half:]), unroll=4)
        st_ref[j] = jnp.concatenate([sr, si], axis=1)
        states = bu_ref[...].reshape(rows, 2 * half).astype(BF16)
        y_ref[:, j * in_lanes:(j + 1) * in_lanes] = _dot(states, cblk_ref[j])
    y = y_ref[...] + dsk_ref[...] * hn_ref[...]
    gb = jax.nn.gelu(y).astype(BF16)
    gl = _dot(gb, wglu_ref[...]) + bglu_ref[...]
    o_ref[...] = (x + gl[:, :d] * jax.nn.sigmoid(gl[:, d:])).reshape(tt, bsz, d)


def _s5_layer(x, norm_w, lam_re, lam_im, log_dt, b_re, b_im, c_re, c_im, d_skip, w_glu, b_glu):
    d = w_glu.shape[0]
    bsz, l = _stream_dims(x, d, True)
    groups, state = lam_re.shape
    gsz = b_re.shape[2]
    nb = S5_LANE_BLOCKS
    gpb = groups // nb
    tt = math.gcd(l, S5_TIME_TILE)

    def expand(a):
        return jnp.repeat(a, gsz, axis=1)

    flat = (groups, state * gsz)
    lbr_e, lbi_e, bbr, bbi = pl.pallas_call(
        _s5_discretize_kernel,
        out_shape=[jax.ShapeDtypeStruct(flat, F32)] * 4,
        name="s5_discretize",
    )(expand(lam_re), expand(lam_im), jnp.broadcast_to(log_dt[:, None], flat),
      b_re.reshape(flat), b_im.reshape(flat))
    lbar_re = lbr_e.reshape(groups, state, gsz)[:, :, 0]
    lbar_im = lbi_e.reshape(groups, state, gsz)[:, :, 0]
    eye = jnp.eye(gpb, dtype=F32)

    def in_block(bb):
        bb = bb.reshape(nb, gpb, state, gsz)
        return jnp.einsum('jgpi,gh->jgihp', bb, eye).reshape(nb, gpb * gsz, gpb * state)

    def out_block(c):
        c = c.reshape(nb, gpb, gsz, state)
        return jnp.einsum('jgip,gh->jhpgi', c, eye).reshape(nb, gpb * state, gpb * gsz)

    bblk = jnp.concatenate([in_block(bbr), in_block(bbi)], axis=2).astype(BF16)
    cblk = jnp.concatenate([out_block(c_re), out_block(-c_im)], axis=1).astype(BF16)
    half = gpb * state
    lre = lbar_re.reshape(nb, 1, half)
    lim = lbar_im.reshape(nb, 1, half)
    rows = tt * bsz
    in_lanes = gpb * gsz
    out = pl.pallas_call(
        _s5_kernel,
        grid=(l // tt,),
        in_specs=[pl.BlockSpec((tt, bsz, d), lambda c: (c, 0, 0)), _const_spec((1, d)),
                  _const_spec((nb, in_lanes, 2 * half)), _const_spec((nb, 2 * half, in_lanes)),
                  _const_spec((nb, 1, half)), _const_spec((nb, 1, half)), _const_spec((1, d)),
                  _const_spec((d, 2 * d)), _const_spec((1, 2 * d))],
        out_specs=pl.BlockSpec((tt, bsz, d), lambda c: (c, 0, 0)),
        out_shape=jax.ShapeDtypeStruct((l, bsz, d), F32),
        scratch_shapes=[pltpu.VMEM((rows, d), F32), pltpu.VMEM((tt, bsz, 2 * half), F32),
                        pltpu.VMEM((rows, d), F32), pltpu.VMEM((nb, bsz, 2 * half), F32)],
        compiler_params=_params(("arbitrary",)),
        name="s5_mixer",
    )(x.reshape(l, bsz, d), norm_w.reshape(1, d), bblk, cblk, lre, lim, d_skip.reshape(1, d),
      w_glu.astype(BF16), b_glu.reshape(1, 2 * d))
    return out.reshape(l, bsz * d)


def _unit_lower_inverse(ms, cs):
    nsys, n, _ = ms.shape
    blk = GDN_INV_BLOCK
    shift = blk.bit_length() - 1
    r = lax.broadcasted_iota(jnp.int32, (n, n), 0)
    c = lax.broadcasted_iota(jnp.int32, (n, n), 1)
    same_block = lax.shift_right_logical(r, shift) == lax.shift_right_logical(c, shift)
    md = jnp.where(same_block, ms, 0.0)
    mn = jnp.where(same_block, 0.0, ms)
    tiles = n // 8
    tiles_per_block = blk // 8
    m_tiles = [md[:, 8 * k:8 * k + 8, :] for k in range(tiles)]
    rows = lax.broadcasted_iota(jnp.int32, (nsys, 8, n), 1)
    cols = lax.broadcasted_iota(jnp.int32, (nsys, 8, n), 2)
    x_tiles = [(rows + 8 * k == cols).astype(F32) for k in range(tiles)]
    for j in range(blk - 1):
        for k in range(tiles):
            b, kk = divmod(k, tiles_per_block)
            if 8 * kk + 7 <= j:
                continue
            src = b * blk + j
            row_j = x_tiles[src // 8][:, src % 8:src % 8 + 1, :]
            x_tiles[k] = x_tiles[k] - m_tiles[k][:, :, src:src + 1] * row_j
    dinv = jnp.concatenate(x_tiles, axis=1)
    systems = range(nsys)
    ps = [_dot(dinv[i].astype(BF16), mn[i].astype(BF16)).astype(BF16) for i in systems]
    ts = [dinv[i] for i in systems]
    for _ in range(cs // blk - 1):
        ts = [dinv[i] - _dot(ps[i], ts[i].astype(BF16)) for i in systems]
    return ts


def _l2norm(t):
    return t * lax.rsqrt(jnp.sum(t * t, axis=-1, keepdims=True) + 1e-6)


def _gdn_kernel(x_ref, nw_ref, wqkv_ref, wgate_ref, wba_ref, cw_ref, alog_ref, dtb_ref, gnw_ref, wo_ref,
                o_ref, pad_ref, st_ref):
    tt = x_ref.shape[0]
    heads, dk, dv = st_ref.shape
    cs = GDN_CHUNK
    nsub = tt // cs
    kd = heads * dk

    @pl.when(pl.program_id(1) == 0)
    def _():
        pad_ref[:, 0:CONV_PAD, :] = jnp.zeros((pad_ref.shape[0], CONV_PAD, pad_ref.shape[2]), F32)
        st_ref[...] = jnp.zeros(st_ref.shape, F32)

    x = x_ref[...]
    hb = _rms(x, nw_ref[...]).astype(BF16)
    ba = _dot(hb, wba_ref[...])
    qkv = _proj_conv_silu(hb, wqkv_ref, cw_ref, None, pad_ref)

    beta = jax.nn.sigmoid(ba)
    gdec = -jnp.exp(alog_ref[...]) * _softplus(ba + dtb_ref[...])
    row = lax.broadcasted_iota(jnp.int32, (tt, tt), 0)
    col = lax.broadcasted_iota(jnp.int32, (tt, tt), 1)
    shift = cs.bit_length() - 1
    same_chunk = lax.shift_right_logical(row, shift) == lax.shift_right_logical(col, shift)
    chunk_tril = jnp.logical_and(row >= col, same_chunk).astype(F32)
    gc = _dot(chunk_tril, gdec, precision=HIGHEST)
    egc = jnp.exp(gc)
    gc_t = gc.T
    chunk_causal = jnp.logical_and(row >= col, same_chunk)
    chunk_strict = jnp.logical_and(row > col, same_chunk)
    last_rows = [gc[(s + 1) * cs - 1:(s + 1) * cs, :] for s in range(nsub)]
    to_end = jnp.exp(jnp.concatenate([last_rows[s] - gc[s * cs:(s + 1) * cs, :] for s in range(nsub)], axis=0))
    e_last = [jnp.exp(last_rows[s]) for s in range(nsub)]
    hs = range(heads)

    q_all, k_all, kb_all, rhs_all, qe_all, kdec_all = [], [], [], [], [], []
    for h in hs:
        lane = heads + h
        qh = _l2norm(qkv[:, h * dk:(h + 1) * dk]) * (dk ** -0.5)
        kh = _l2norm(qkv[:, kd + h * dk:kd + (h + 1) * dk])
        vh = qkv[:, 2 * kd + h * dv:2 * kd + (h + 1) * dv]
        bcol = beta[:, h:h + 1]
        e_col = egc[:, lane:lane + 1]
        kbh = kh * bcol
        q_all.append(qh)
        k_all.append(kh)
        kb_all.append(kbh)
        rhs_all.append(jnp.concatenate([vh * bcol, kbh * e_col], axis=1).astype(BF16))
        qe_all.append(qh * e_col)
        kdec_all.append(kh * to_end[:, lane:lane + 1])
    g_all = [_dot_nt(jnp.concatenate([kb_all[h], q_all[h]], axis=0).astype(BF16), k_all[h].astype(BF16))
             for h in hs]
    ms, attn_all = [], []
    for h in hs:
        lane = heads + h
        seg = gc[:, lane:lane + 1] - gc_t[lane:lane + 1, :]
        decay = jnp.exp(jnp.where(chunk_causal, seg, -jnp.inf))
        ms.append(jnp.where(chunk_strict, g_all[h][:tt] * decay, 0.0))
        attn_all.append((g_all[h][tt:] * decay).astype(BF16))
    t_all = _unit_lower_inverse(jnp.stack(ms), cs)
    sol_all = [_dot(t_all[h].astype(BF16), rhs_all[h]) for h in hs]

    states = [st_ref[h] for h in hs]
    v_new_all = [[] for _ in hs]
    o_off_all = [[] for _ in hs]
    for s in range(nsub):
        sl = slice(s * cs, (s + 1) * cs)
        ws_all = [_dot(jnp.concatenate([sol_all[h][sl, dv:], qe_all[h][sl]], axis=0).astype(BF16),
                       states[h].astype(BF16)) for h in hs]
        k_dec_t = [kdec_all[h][sl].T.astype(BF16) for h in hs]
        for h in hs:
            v_new_all[h].append((sol_all[h][sl, :dv] - ws_all[h][:cs]).astype(BF16))
            o_off_all[h].append(ws_all[h][cs:])
        states = [states[h] * e_last[s][:, heads + h:heads + h + 1] + _dot(k_dec_t[h], v_new_all[h][s])
                  for h in hs]
    for h in hs:
        st_ref[h] = states[h]
    o_in_all = [_dot(attn_all[h], jnp.concatenate(v_new_all[h], axis=0)) for h in hs]
    gate = _dot(hb, wgate_ref[...])
    o_heads = []
    for h in hs:
        oh = jnp.concatenate(o_off_all[h], axis=0) + o_in_all[h]
        oh = oh * lax.rsqrt(jnp.mean(oh * oh, axis=-1, keepdims=True) + NORM_EPS)
        oh = oh * gnw_ref[...] * _silu(gate[:, h * dv:(h + 1) * dv])
        o_heads.append(oh.astype(BF16))
    o_ref[...] = x + _dot(jnp.concatenate(o_heads, axis=1), wo_ref[...])


def _gdn_layer(x, norm_w, w_in, conv_w, a_log, dt_bias, gn_w, w_out):
    d = w_in.shape[0]
    bsz, l = _stream_dims(x, d, True)
    dk, dv = GDN_DK, GDN_DV
    heads = w_out.shape[0] // dv
    conv_ch = heads * (2 * dk + dv)
    off = conv_ch + heads * dv
    tt = math.gcd(l, GDN_TIME_TILE)
    lane_pad = LANES - 2 * heads
    wqkv = w_in[:, :conv_ch].astype(BF16)
    wgate = w_in[:, conv_ch:off].astype(BF16)
    wba = jnp.pad(w_in[:, off:], ((0, 0), (0, lane_pad))).astype(BF16)
    alog = jnp.pad(a_log, (heads, lane_pad)).reshape(1, LANES)
    dtb = jnp.pad(dt_bias, (heads, lane_pad)).reshape(1, LANES)
    return pl.pallas_call(
        _gdn_kernel,
        grid=(bsz, l // tt),
        in_specs=[_row_spec(tt, d, True), _const_spec((1, d)), _const_spec((d, conv_ch)),
                  _const_spec((d, heads * dv)), _const_spec((d, LANES)), _const_spec((CONV_K, conv_ch)),
                  _const_spec((1, LANES)), _const_spec((1, LANES)), _const_spec((1, dv)),
                  _const_spec((heads * dv, d))],
        out_specs=_row_spec(tt, d, True),
        out_shape=jax.ShapeDtypeStruct(_stream_shape(bsz, l, d, True), F32),
        scratch_shapes=[pltpu.VMEM((conv_ch // CONV_BLOCK, CONV_PAD + tt, CONV_BLOCK), F32),
                        pltpu.VMEM((heads, dk, dv), F32)],
        compiler_params=_params(("parallel", "arbitrary")),
        name="gdn_mixer",
    )(x, norm_w.reshape(1, d), wqkv, wgate, wba, conv_w, alog, dtb, gn_w.reshape(1, dv), w_out.astype(BF16))


def kernel(x, mix_norm_w, mamba_w_in, mamba_conv_w, mamba_conv_b, mamba_dt_bias, mamba_a_log, mamba_d, mamba_norm_w, mamba_w_out, s5_lam_re, s5_lam_im, s5_log_dt, s5_b_re, s5_b_im, s5_c_re, s5_c_im, s5_d, s5_w_glu, s5_b_glu, gdn_w_in, gdn_conv_w, gdn_a_log, gdn_dt_bias, gdn_norm_w, gdn_w_out, ffn_norm_w, ffn_w_in, ffn_w_out, final_norm_w):
    depth = mix_norm_w.shape[0]
    h = x
    for i in range(depth):
        kind, j = i % 3, i // 3
        if kind == 0:
            h = _mamba_layer(h, mix_norm_w[i], mamba_w_in[j], mamba_conv_w[j], mamba_conv_b[j], mamba_dt_bias[j],
                             mamba_a_log[j], mamba_d[j], mamba_norm_w[j], mamba_w_out[j],
                             in_tm=i > 0, out_tm=True)
        elif kind == 1:
            h = _s5_layer(h, mix_norm_w[i], s5_lam_re[j], s5_lam_im[j], s5_log_dt[j], s5_b_re[j], s5_b_im[j],
                          s5_c_re[j], s5_c_im[j], s5_d[j], s5_w_glu[j], s5_b_glu[j])
        else:
            h = _gdn_layer(h, mix_norm_w[i], gdn_w_in[j], gdn_conv_w[j], gdn_a_log[j], gdn_dt_bias[j],
                           gdn_norm_w[j], gdn_w_out[j])
        last = i == depth - 1
        h = _ffn_layer(h, ffn_norm_w[i], ffn_w_in[i], ffn_w_out[i], final_norm_w,
                       in_tm=True, out_tm=not last, final_norm=last)
    return h
```

```python
import functools
import math

import jax
import jax.numpy as jnp
from jax import lax
from jax.experimental import pallas as pl
from jax.experimental.pallas import tpu as pltpu

F32 = jnp.float32
BF16 = jnp.bfloat16
HIGHEST = lax.Precision.HIGHEST

NORM_EPS = 1e-6
CONV_K = 4
CONV_PAD = 8
CONV_BLOCK = 512
LANES = 128
VMEM_LIMIT_BYTES = 56 * 1024 * 1024

MAMBA_HEADDIM = 64
MAMBA_GROUPS = 8
MAMBA_D_STATE = 128
MAMBA_CHUNK = 256

S5_GROUP_SIZE = 16
S5_STATE = 64
S5_TIME_TILE = 64
S5_LANE_BLOCKS = 8

GDN_DK = 128
GDN_DV = 256
GDN_CHUNK = 64
GDN_INV_BLOCK = 16
GDN_TIME_TILE = 256

FFN_ROW_TILE = 1024


def _dot(a, b, **kw):
    return jnp.dot(a, b, preferred_element_type=F32, **kw)


def _dot_nt(a, b):
    return lax.dot_general(a, b, (((1,), (1,)), ((), ())), preferred_element_type=F32)


def _rms(x, w):
    return x * lax.rsqrt(jnp.mean(x * x, axis=-1, keepdims=True) + NORM_EPS) * w


def _softplus(x):
    return jnp.maximum(x, 0.0) + jnp.log1p(jnp.exp(-jnp.abs(x)))


def _silu(x):
    h = 0.5 * x
    return h + h * jnp.tanh(h)


def _const_spec(shape):
    nd = len(shape)
    return pl.BlockSpec(shape, lambda *_: (0,) * nd, pipeline_mode=pl.Buffered(1))


def _row_spec(tile, d):
    return pl.BlockSpec((None, tile, d), lambda b, c: (b, c, 0))


def _pad_lanes(w):
    return jnp.pad(w, ((0, 0), (0, -w.shape[1] % LANES)))


def _params(semantics):
    return pltpu.CompilerParams(dimension_semantics=semantics, vmem_limit_bytes=VMEM_LIMIT_BYTES)


def _ffn_kernel(x_ref, nw_ref, win_ref, wo_ref, fw_ref, o_ref, *, hidden_chunk, final_norm):
    x = x_ref[...]
    hb = _rms(x, nw_ref[...]).astype(BF16)
    acc = x
    hidden = wo_ref.shape[0]
    for c0 in range(0, hidden, hidden_chunk):
        g = _dot(hb, win_ref[:, c0:c0 + hidden_chunk])
        u = _dot(hb, win_ref[:, hidden + c0:hidden + c0 + hidden_chunk])
        a = (_silu(g) * u).astype(BF16)
        acc = acc + _dot(a, wo_ref[c0:c0 + hidden_chunk, :])
    if final_norm:
        acc = _rms(acc, fw_ref[...])
    o_ref[...] = acc


def _ffn_layer(x, norm_w, w_in, w_out, final_w, *, final_norm):
    bsz, l, d = x.shape
    hidden = w_out.shape[0]
    hidden_chunk = hidden // 2 if (hidden // 2) % LANES == 0 else hidden
    tile = min(FFN_ROW_TILE, l)
    kern = functools.partial(_ffn_kernel, hidden_chunk=hidden_chunk, final_norm=final_norm)
    return pl.pallas_call(
        kern,
        grid=(bsz, l // tile),
        in_specs=[_row_spec(tile, d), _const_spec((1, d)), _const_spec((d, 2 * hidden)),
                  _const_spec((hidden, d)), _const_spec((1, d))],
        out_specs=_row_spec(tile, d),
        out_shape=jax.ShapeDtypeStruct((bsz, l, d), F32),
        compiler_params=_params(("parallel", "parallel")),
        name="swiglu_ffn",
    )(x, norm_w.reshape(1, d), w_in.astype(BF16), w_out.astype(BF16), final_w.reshape(1, d))


def _proj_conv_silu(hb, w_ref, w_col0, cw_ref, cb_ref, pad_ref):
    rows = hb.shape[0]
    parts = []
    for j in range(pad_ref.shape[0]):
        c0 = j * CONV_BLOCK
        xb = _dot(hb, w_ref[:, w_col0 + c0:w_col0 + c0 + CONV_BLOCK])
        xpad = jnp.concatenate([pad_ref[j], xb], axis=0)
        conv = cw_ref[CONV_K - 1:CONV_K, c0:c0 + CONV_BLOCK] * xb
        if cb_ref is not None:
            conv = conv + cb_ref[:, c0:c0 + CONV_BLOCK]
        for k in range(CONV_K - 1):
            s = CONV_K - 1 - k
            conv = conv + cw_ref[k:k + 1, c0:c0 + CONV_BLOCK] * pltpu.roll(xpad, s, 0)[CONV_PAD:, :]
        pad_ref[j] = xb[rows - CONV_PAD:rows, :]
        parts.append(_silu(conv))
    return jnp.concatenate(parts, axis=1)


def _pair_cols(mat, h0, lane_lt64):
    return jnp.where(lane_lt64, mat[:, h0:h0 + 1], mat[:, h0 + 1:h0 + 2])


def _mamba_kernel(x_ref, nw_ref, win_ref, cw_ref, cb_ref, dtb_ref, alog_ref,
                  dsk_ref, gnw_ref, wo_ref, o_ref, pad_ref, st_ref):
    q = x_ref.shape[0]
    d_inner = wo_ref.shape[0]
    conv_ch = cw_ref.shape[1]
    n = MAMBA_D_STATE
    gs = d_inner // MAMBA_GROUPS
    heads_per_group = gs // MAMBA_HEADDIM

    @pl.when(pl.program_id(1) == 0)
    def _():
        pad_ref[...] = jnp.zeros(pad_ref.shape, F32)
        st_ref[...] = jnp.zeros(st_ref.shape, F32)

    x = x_ref[...]
    hb = _rms(x, nw_ref[...]).astype(BF16)
    dtr = _dot(hb, win_ref[:, d_inner + conv_ch:])
    xbc = _proj_conv_silu(hb, win_ref, d_inner, cw_ref, cb_ref, pad_ref)
    xs = xbc[:, :d_inner]
    bm = xbc[:, d_inner:d_inner + MAMBA_GROUPS * n]
    cm = xbc[:, d_inner + MAMBA_GROUPS * n:]

    dt = _softplus(dtr + dtb_ref[...])
    da = dt * (-jnp.exp(alog_ref[...]))
    row = lax.broadcasted_iota(jnp.int32, (q, q), 0)
    col = lax.broadcasted_iota(jnp.int32, (q, q), 1)
    causal = row >= col
    acs = _dot(causal.astype(F32), da, precision=HIGHEST)
    acs_t = acs.T
    dt_t = dt.T
    eacs = jnp.exp(acs)
    alast = acs[q - 1:q, :]
    toend = jnp.exp(alast - acs) * dt
    elast = jnp.exp(alast)
    lane_lt64 = lax.broadcasted_iota(jnp.int32, (1, LANES), 1) < MAMBA_HEADDIM

    xsb = xs.astype(BF16)
    bmb = bm.astype(BF16)
    cmb = cm.astype(BF16)
    zero_pair = jnp.zeros((q, LANES), BF16)
    y_groups = []
    for g in range(MAMBA_GROUPS):
        bg = bmb[:, g * n:(g + 1) * n]
        cg = cmb[:, g * n:(g + 1) * n]
        cb = jnp.where(causal, _dot_nt(cg, bg), 0.0)
        state = st_ref[g]
        y_off = _dot(cg, state.astype(BF16))
        y_pairs, e_pairs, w_pairs, l_pairs = [], [], [], []
        for pr in range(heads_per_group // 2):
            h0 = g * heads_per_group + 2 * pr
            lane0 = h0 * MAMBA_HEADDIM
            x_pair = xsb[:, lane0:lane0 + LANES]
            lms = []
            for h in (h0, h0 + 1):
                blocks = []
                for r0 in range(0, q, LANES):
                    r1 = r0 + LANES
                    seg = acs[r0:r1, h:h + 1] - acs_t[h:h + 1, :r1]
                    lm = (cb[r0:r1, :r1] * jnp.exp(jnp.minimum(seg, 0.0)) * dt_t[h:h + 1, :r1]).astype(BF16)
                    if r1 < q:
                        lm = jnp.concatenate([lm, jnp.zeros((LANES, q - r1), BF16)], axis=1)
                    blocks.append(lm)
                lms.append(jnp.concatenate(blocks, axis=0))
            lhs = jnp.concatenate(lms, axis=1)
            rhs = jnp.concatenate([jnp.where(lane_lt64, x_pair, zero_pair),
                                   jnp.where(lane_lt64, zero_pair, x_pair)], axis=0)
            y_pairs.append(_dot(lhs, rhs))
            e_pairs.append(_pair_cols(eacs, h0, lane_lt64))
            w_pairs.append(_pair_cols(toend, h0, lane_lt64))
            l_pairs.append(_pair_cols(elast, h0, lane_lt64))
        y_diag = jnp.concatenate(y_pairs, axis=1)
        e_exp = jnp.concatenate(e_pairs, axis=1)
        w_exp = jnp.concatenate(w_pairs, axis=1)
        l_exp = jnp.concatenate(l_pairs, axis=1)
        xs_g = xs[:, g * gs:(g + 1) * gs]
        y_groups.append(y_diag + y_off * e_exp + dsk_ref[:, g * gs:(g + 1) * gs] * xs_g)
        bg_t = bm[:, g * n:(g + 1) * n].T.astype(BF16)
        st_ref[g] = state * l_exp + _dot(bg_t, (xs_g * w_exp).astype(BF16))

    z = _dot(hb, win_ref[:, :d_inner])
    outs = []
    for g in range(MAMBA_GROUPS):
        yg = y_groups[g] * _silu(z[:, g * gs:(g + 1) * gs])
        yg = yg * lax.rsqrt(jnp.mean(yg * yg, axis=-1, keepdims=True) + NORM_EPS)
        outs.append((yg * gnw_ref[:, g * gs:(g + 1) * gs]).astype(BF16))
    y = jnp.concatenate(outs, axis=1)
    o_ref[...] = x + _dot(y, wo_ref[...])


def _mamba_layer(x, norm_w, w_in, conv_w, conv_b, dt_bias, a_log, d_skip, gn_w, w_out):
    bsz, l, d = x.shape
    d_inner = w_out.shape[0]
    heads = d_inner // MAMBA_HEADDIM
    conv_ch = d_inner + 2 * MAMBA_GROUPS * MAMBA_D_STATE
    gs = d_inner // MAMBA_GROUPS
    q = math.gcd(l, MAMBA_CHUNK)
    pad_h = LANES - heads
    win = _pad_lanes(w_in).astype(BF16)
    dtb = jnp.pad(dt_bias, (0, pad_h)).reshape(1, LANES)
    alog = jnp.pad(a_log, (0, pad_h)).reshape(1, LANES)
    dsk = jnp.repeat(d_skip, MAMBA_HEADDIM).reshape(1, d_inner)
    return pl.pallas_call(
        _mamba_kernel,
        grid=(bsz, l // q),
        in_specs=[_row_spec(q, d), _const_spec((1, d)), _const_spec(win.shape), _const_spec((CONV_K, conv_ch)),
                  _const_spec((1, conv_ch)), _const_spec((1, LANES)), _const_spec((1, LANES)),
                  _const_spec((1, d_inner)), _const_spec((1, d_inner)), _const_spec((d_inner, d))],
        out_specs=_row_spec(q, d),
        out_shape=jax.ShapeDtypeStruct((bsz, l, d), F32),
        scratch_shapes=[pltpu.VMEM((conv_ch // CONV_BLOCK, CONV_PAD, CONV_BLOCK), F32),
                        pltpu.VMEM((MAMBA_GROUPS, MAMBA_D_STATE, gs), F32)],
        compiler_params=_params(("parallel", "arbitrary")),
        name="mamba2_mixer",
    )(x, norm_w.reshape(1, d), win, conv_w, conv_b.reshape(1, conv_ch), dtb, alog, dsk,
      gn_w.reshape(1, d_inner), w_out.astype(BF16))


def _s5_discretize_kernel(lr_ref, li_ref, logdt_ref, bre_ref, bim_ref, lbr_ref, lbi_ref, bbr_ref, bbi_ref):
    lr = lr_ref[...]
    li = li_ref[...]
    dt = jnp.exp(logdt_ref[...])
    mag = jnp.exp(lr * dt)
    lbr = mag * jnp.cos(li * dt)
    lbi = mag * jnp.sin(li * dt)
    den = lr * lr + li * li
    zr = ((lbr - 1.0) * lr + lbi * li) / den
    zi = (lbi * lr - (lbr - 1.0) * li) / den
    lbr_ref[...] = lbr
    lbi_ref[...] = lbi
    bbr_ref[...] = zr * bre_ref[...] - zi * bim_ref[...]
    bbi_ref[...] = zr * bim_ref[...] + zi * bre_ref[...]


def _s5_kernel(x_ref, nw_ref, bblk_ref, cblk_ref, lre_ref, lim_ref, dsk_ref, wglu_ref, bglu_ref, o_ref,
               hn_ref, y_ref, st_ref):
    bsz, tt, d = x_ref.shape
    half = st_ref.shape[2] // 2
    n_blocks = bblk_ref.shape[0]
    in_lanes = bblk_ref.shape[1]

    @pl.when(pl.program_id(0) == 0)
    def _():
        st_ref[...] = jnp.zeros(st_ref.shape, F32)

    for b in range(bsz):
        hn_b = _rms(x_ref[b], nw_ref[...])
        for j in range(n_blocks):
            hn_ref[j, pl.ds(b, tt, stride=bsz), :] = hn_b[:, j * in_lanes:(j + 1) * in_lanes]
    for j in range(n_blocks):
        bu = _dot(hn_ref[j].astype(BF16), bblk_ref[j])
        ar = jnp.broadcast_to(lre_ref[j], (bsz, half))
        ai = jnp.broadcast_to(lim_ref[j], (bsz, half))
        st = st_ref[j]
        sr, si = st[:, :half], st[:, half:]
        states = []
        for t in range(tt):
            v = bu[t * bsz:(t + 1) * bsz]
            sr, si = ar * sr - ai * si + v[:, :half], ar * si + ai * sr + v[:, half:]
            states.append(jnp.concatenate([sr, si], axis=1).astype(BF16))
        st_ref[j] = jnp.concatenate([sr, si], axis=1)
        y_ref[j] = (_dot(jnp.concatenate(states, axis=0), cblk_ref[j])
                    + dsk_ref[:, j * in_lanes:(j + 1) * in_lanes] * hn_ref[j])
    y = jnp.concatenate([y_ref[j] for j in range(n_blocks)], axis=1)
    gb = jax.nn.gelu(y).astype(BF16)
    gl = _dot(gb, wglu_ref[...]) + bglu_ref[...]
    res = gl[:, :d] * jax.nn.sigmoid(gl[:, d:])
    for j in range(n_blocks):
        y_ref[j] = res[:, j * in_lanes:(j + 1) * in_lanes]
    for b in range(bsz):
        mix_b = jnp.concatenate([y_ref[j, pl.ds(b, tt, stride=bsz), :] for j in range(n_blocks)], axis=1)
        o_ref[b] = x_ref[b] + mix_b


def _s5_layer(x, norm_w, lam_re, lam_im, log_dt, b_re, b_im, c_re, c_im, d_skip, w_glu, b_glu):
    bsz, l, d = x.shape
    groups, state = lam_re.shape
    gsz = b_re.shape[2]
    nb = S5_LANE_BLOCKS
    gpb = groups // nb
    tt = math.gcd(l, S5_TIME_TILE)

    def expand(a):
        return jnp.repeat(a, gsz, axis=1)

    flat = (groups, state * gsz)
    lbr_e, lbi_e, bbr, bbi = pl.pallas_call(
        _s5_discretize_kernel,
        out_shape=[jax.ShapeDtypeStruct(flat, F32)] * 4,
        name="s5_discretize",
    )(expand(lam_re), expand(lam_im), jnp.broadcast_to(log_dt[:, None], flat),
      b_re.reshape(flat), b_im.reshape(flat))
    lbar_re = lbr_e.reshape(groups, state, gsz)[:, :, 0]
    lbar_im = lbi_e.reshape(groups, state, gsz)[:, :, 0]
    eye = jnp.eye(gpb, dtype=F32)

    def in_block(bb):
        bb = bb.reshape(nb, gpb, state, gsz)
        return jnp.einsum('jgpi,gh->jgihp', bb, eye).reshape(nb, gpb * gsz, gpb * state)

    def out_block(c):
        c = c.reshape(nb, gpb, gsz, state)
        return jnp.einsum('jgip,gh->jhpgi', c, eye).reshape(nb, gpb * state, gpb * gsz)

    bblk = jnp.concatenate([in_block(bbr), in_block(bbi)], axis=2).astype(BF16)
    cblk = jnp.concatenate([out_block(c_re), out_block(-c_im)], axis=1).astype(BF16)
    half = gpb * state
    lre = lbar_re.reshape(nb, 1, half)
    lim = lbar_im.reshape(nb, 1, half)
    rows = tt * bsz
    in_lanes = gpb * gsz
    return pl.pallas_call(
        _s5_kernel,
        grid=(l // tt,),
        in_specs=[pl.BlockSpec((bsz, tt, d), lambda c: (0, c, 0)), _const_spec((1, d)),
                  _const_spec((nb, in_lanes, 2 * half)), _const_spec((nb, 2 * half, in_lanes)),
                  _const_spec((nb, 1, half)), _const_spec((nb, 1, half)), _const_spec((1, d)),
                  _const_spec((d, 2 * d)), _const_spec((1, 2 * d))],
        out_specs=pl.BlockSpec((bsz, tt, d), lambda c: (0, c, 0)),
        out_shape=jax.ShapeDtypeStruct((bsz, l, d), F32),
        scratch_shapes=[pltpu.VMEM((nb, rows, in_lanes), F32), pltpu.VMEM((nb, rows, in_lanes), F32),
                        pltpu.VMEM((nb, bsz, 2 * half), F32)],
        compiler_params=_params(("arbitrary",)),
        name="s5_mixer",
    )(x, norm_w.reshape(1, d), bblk, cblk, lre, lim, d_skip.reshape(1, d),
      w_glu.astype(BF16), b_glu.reshape(1, 2 * d))


def _unit_lower_inverse(ms, cs):
    nsys, n, _ = ms.shape
    blk = GDN_INV_BLOCK
    shift = blk.bit_length() - 1
    r = lax.broadcasted_iota(jnp.int32, (n, n), 0)
    c = lax.broadcasted_iota(jnp.int32, (n, n), 1)
    same_block = lax.shift_right_logical(r, shift) == lax.shift_right_logical(c, shift)
    md = jnp.where(same_block, ms, 0.0)
    mn = jnp.where(same_block, 0.0, ms)
    tiles = n // 8
    tiles_per_block = blk // 8
    m_tiles = [md[:, 8 * k:8 * k + 8, :] for k in range(tiles)]
    rows = lax.broadcasted_iota(jnp.int32, (nsys, 8, n), 1)
    cols = lax.broadcasted_iota(jnp.int32, (nsys, 8, n), 2)
    x_tiles = [(rows + 8 * k == cols).astype(F32) for k in range(tiles)]
    for j in range(blk - 1):
        for k in range(tiles):
            b, kk = divmod(k, tiles_per_block)
            if 8 * kk + 7 <= j:
                continue
            src = b * blk + j
            row_j = x_tiles[src // 8][:, src % 8:src % 8 + 1, :]
            x_tiles[k] = x_tiles[k] - m_tiles[k][:, :, src:src + 1] * row_j
    dinv = jnp.concatenate(x_tiles, axis=1)
    systems = range(nsys)
    ps = [_dot(dinv[i].astype(BF16), mn[i].astype(BF16)).astype(BF16) for i in systems]
    ts = [dinv[i] for i in systems]
    for _ in range(cs // blk - 1):
        ts = [dinv[i] - _dot(ps[i], ts[i].astype(BF16)) for i in systems]
    return ts


def _l2norm(t):
    return t * lax.rsqrt(jnp.sum(t * t, axis=-1, keepdims=True) + 1e-6)


def _gdn_kernel(x_ref, nw_ref, win_ref, cw_ref, alog_ref, dtb_ref, gnw_ref, wo_ref,
                o_ref, pad_ref, st_ref):
    tt = x_ref.shape[0]
    heads, dk, dv = st_ref.shape
    cs = GDN_CHUNK
    nsub = tt // cs
    kd = heads * dk
    conv_ch = cw_ref.shape[1]
    gate_end = conv_ch + heads * dv

    @pl.when(pl.program_id(1) == 0)
    def _():
        pad_ref[...] = jnp.zeros(pad_ref.shape, F32)
        st_ref[...] = jnp.zeros(st_ref.shape, F32)

    x = x_ref[...]
    hb = _rms(x, nw_ref[...]).astype(BF16)
    ba = _dot(hb, win_ref[:, gate_end:])
    qkv = _proj_conv_silu(hb, win_ref, 0, cw_ref, None, pad_ref)

    beta = jax.nn.sigmoid(ba)
    gdec = -jnp.exp(alog_ref[...]) * _softplus(ba + dtb_ref[...])
    shift = cs.bit_length() - 1

    def chunk_masks(size):
        row = lax.broadcasted_iota(jnp.int32, (size, size), 0)
        col = lax.broadcasted_iota(jnp.int32, (size, size), 1)
        same_chunk = lax.shift_right_logical(row, shift) == lax.shift_right_logical(col, shift)
        return jnp.logical_and(row >= col, same_chunk), jnp.logical_and(row > col, same_chunk)

    gc = _dot(chunk_masks(tt)[0].astype(F32), gdec, precision=HIGHEST)
    egc = jnp.exp(gc)
    gc_t = gc.T
    sb = min(tt, LANES)
    nsb = tt // sb
    chunk_causal, chunk_strict = chunk_masks(sb)
    last_rows = [gc[(s + 1) * cs - 1:(s + 1) * cs, :] for s in range(nsub)]
    to_end = jnp.exp(jnp.concatenate([last_rows[s] - gc[s * cs:(s + 1) * cs, :] for s in range(nsub)], axis=0))
    e_last = [jnp.exp(last_rows[s]) for s in range(nsub)]
    hs = range(heads)

    q_all, k_all, kb_all, rhs_all, qe_all, kdec_all = [], [], [], [], [], []
    for h in hs:
        lane = heads + h
        qh = _l2norm(qkv[:, h * dk:(h + 1) * dk]) * (dk ** -0.5)
        kh = _l2norm(qkv[:, kd + h * dk:kd + (h + 1) * dk])
        vh = qkv[:, 2 * kd + h * dv:2 * kd + (h + 1) * dv]
        bcol = beta[:, h:h + 1]
        e_col = egc[:, lane:lane + 1]
        kbh = kh * bcol
        q_all.append(qh)
        k_all.append(kh)
        kb_all.append(kbh)
        rhs_all.append(jnp.concatenate([vh * bcol, kbh * e_col], axis=1).astype(BF16))
        qe_all.append(qh * e_col)
        kdec_all.append(kh * to_end[:, lane:lane + 1])
    hsb = [(h, a) for h in hs for a in range(nsb)]
    g_all = {(h, a): _dot_nt(jnp.concatenate([kb_all[h][a * sb:(a + 1) * sb], q_all[h][a * sb:(a + 1) * sb]],
                                             axis=0).astype(BF16), k_all[h][a * sb:(a + 1) * sb].astype(BF16))
             for h, a in hsb}
    ms, attn_all = [], {}
    for h, a in hsb:
        lane = heads + h
        seg = gc[a * sb:(a + 1) * sb, lane:lane + 1] - gc_t[lane:lane + 1, a * sb:(a + 1) * sb]
        decay = jnp.exp(jnp.where(chunk_causal, seg, -jnp.inf))
        ms.append(jnp.where(chunk_strict, g_all[h, a][:sb] * decay, 0.0))
        attn_all[h, a] = (g_all[h, a][sb:] * decay).astype(BF16)
    t_all = _unit_lower_inverse(jnp.stack(ms), cs)
    sol_parts = {(h, a): _dot(t_all[h * nsb + a].astype(BF16), rhs_all[h][a * sb:(a + 1) * sb]) for h, a in hsb}
    sol_all = [jnp.concatenate([sol_parts[h, a] for a in range(nsb)], axis=0) for h in hs]

    states = [st_ref[h] for h in hs]
    v_new_all = [[] for _ in hs]
    o_off_all = [[] for _ in hs]
    for s in range(nsub):
        sl = slice(s * cs, (s + 1) * cs)
        ws_all = [_dot(jnp.concatenate([sol_all[h][sl, dv:], qe_all[h][sl]], axis=0).astype(BF16),
                       states[h].astype(BF16)) for h in hs]
        k_dec_t = [kdec_all[h][sl].T.astype(BF16) for h in hs]
        for h in hs:
            v_new_all[h].append((sol_all[h][sl, :dv] - ws_all[h][:cs]).astype(BF16))
            o_off_all[h].append(ws_all[h][cs:])
        states = [states[h] * e_last[s][:, heads + h:heads + h + 1] + _dot(k_dec_t[h], v_new_all[h][s])
                  for h in hs]
    for h in hs:
        st_ref[h] = states[h]
    v_new_cat = [jnp.concatenate(v_new_all[h], axis=0) for h in hs]
    o_in_parts = {(h, a): _dot(attn_all[h, a], v_new_cat[h][a * sb:(a + 1) * sb]) for h, a in hsb}
    o_in_all = [jnp.concatenate([o_in_parts[h, a] for a in range(nsb)], axis=0) for h in hs]
    gate = _dot(hb, win_ref[:, conv_ch:gate_end])
    o_heads = []
    for h in hs:
        oh = jnp.concatenate(o_off_all[h], axis=0) + o_in_all[h]
        oh = oh * lax.rsqrt(jnp.mean(oh * oh, axis=-1, keepdims=True) + NORM_EPS)
        oh = oh * gnw_ref[...] * _silu(gate[:, h * dv:(h + 1) * dv])
        o_heads.append(oh.astype(BF16))
    o_ref[...] = x + _dot(jnp.concatenate(o_heads, axis=1), wo_ref[...])


def _gdn_layer(x, norm_w, w_in, conv_w, a_log, dt_bias, gn_w, w_out):
    bsz, l, d = x.shape
    dk, dv = GDN_DK, GDN_DV
    heads = w_out.shape[0] // dv
    conv_ch = heads * (2 * dk + dv)
    tt = math.gcd(l, GDN_TIME_TILE)
    lane_pad = LANES - 2 * heads
    win = _pad_lanes(w_in).astype(BF16)
    alog = jnp.pad(a_log, (heads, lane_pad)).reshape(1, LANES)
    dtb = jnp.pad(dt_bias, (heads, lane_pad)).reshape(1, LANES)
    return pl.pallas_call(
        _gdn_kernel,
        grid=(bsz, l // tt),
        in_specs=[_row_spec(tt, d), _const_spec((1, d)), _const_spec(win.shape), _const_spec((CONV_K, conv_ch)),
                  _const_spec((1, LANES)), _const_spec((1, LANES)), _const_spec((1, dv)),
                  _const_spec((heads * dv, d))],
        out_specs=_row_spec(tt, d),
        out_shape=jax.ShapeDtypeStruct((bsz, l, d), F32),
        scratch_shapes=[pltpu.VMEM((conv_ch // CONV_BLOCK, CONV_PAD, CONV_BLOCK), F32),
                        pltpu.VMEM((heads, dk, dv), F32)],
        compiler_params=_params(("parallel", "arbitrary")),
        name="gdn_mixer",
    )(x, norm_w.reshape(1, d), win, conv_w, alog, dtb, gn_w.reshape(1, dv), w_out.astype(BF16))


def kernel(x, mix_norm_w, mamba_w_in, mamba_conv_w, mamba_conv_b, mamba_dt_bias, mamba_a_log, mamba_d, mamba_norm_w, mamba_w_out, s5_lam_re, s5_lam_im, s5_log_dt, s5_b_re, s5_b_im, s5_c_re, s5_c_im, s5_d, s5_w_glu, s5_b_glu, gdn_w_in, gdn_conv_w, gdn_a_log, gdn_dt_bias, gdn_norm_w, gdn_w_out, ffn_norm_w, ffn_w_in, ffn_w_out, final_norm_w):
    depth = mix_norm_w.shape[0]
    h = x
    for i in range(depth):
        kind, j = i % 3, i // 3
        if kind == 0:
            h = _mamba_layer(h, mix_norm_w[i], mamba_w_in[j], mamba_conv_w[j], mamba_conv_b[j], mamba_dt_bias[j],
                             mamba_a_log[j], mamba_d[j], mamba_norm_w[j], mamba_w_out[j])
        elif kind == 1:
            h = _s5_layer(h, mix_norm_w[i], s5_lam_re[j], s5_lam_im[j], s5_log_dt[j], s5_b_re[j], s5_b_im[j],
                          s5_c_re[j], s5_c_im[j], s5_d[j], s5_w_glu[j], s5_b_glu[j])
        else:
            h = _gdn_layer(h, mix_norm_w[i], gdn_w_in[j], gdn_conv_w[j], gdn_a_log[j], gdn_dt_bias[j],
                           gdn_norm_w[j], gdn_w_out[j])
        h = _ffn_layer(h, ffn_norm_w[i], ffn_w_in[i], ffn_w_out[i], final_norm_w, final_norm=i == depth - 1)
    return h
```

```python
import functools
import math

import jax
import jax.numpy as jnp
from jax import lax
from jax.experimental import pallas as pl
from jax.experimental.pallas import tpu as pltpu

F32 = jnp.float32
BF16 = jnp.bfloat16
HIGHEST = lax.Precision.HIGHEST

NORM_EPS = 1e-6
CONV_K = 4
CONV_PAD = 8
CONV_BLOCK = 512
LANES = 128
VMEM_LIMIT_BYTES = 56 * 1024 * 1024

MAMBA_HEADDIM = 64
MAMBA_GROUPS = 8
MAMBA_D_STATE = 128
MAMBA_CHUNK = 256

S5_GROUP_SIZE = 16
S5_STATE = 64
S5_TIME_TILE = 64
S5_LANE_BLOCKS = 8

GDN_DK = 128
GDN_DV = 256
GDN_CHUNK = 64
GDN_INV_BLOCK = 16
GDN_TIME_TILE = 256

FFN_ROW_TILE = 1024


def _dot(a, b, **kw):
    return jnp.dot(a, b, preferred_element_type=F32, **kw)


def _dot_nt(a, b):
    return lax.dot_general(a, b, (((1,), (1,)), ((), ())), preferred_element_type=F32)


def _rms(x, w):
    return x * lax.rsqrt(jnp.mean(x * x, axis=-1, keepdims=True) + NORM_EPS) * w


def _softplus(x):
    return jnp.maximum(x, 0.0) + jnp.log1p(jnp.exp(-jnp.abs(x)))


def _silu(x):
    h = 0.5 * x
    return h + h * jnp.tanh(h)


def _const_spec(shape):
    nd = len(shape)
    return pl.BlockSpec(shape, lambda *_: (0,) * nd, pipeline_mode=pl.Buffered(1))


def _row_spec(tile, d):
    return pl.BlockSpec((None, tile, d), lambda b, c: (b, c, 0))


def _pad_lanes(w):
    return jnp.pad(w, ((0, 0), (0, -w.shape[1] % LANES)))


def _params(semantics):
    return pltpu.CompilerParams(dimension_semantics=semantics, vmem_limit_bytes=VMEM_LIMIT_BYTES)


def _ffn_kernel(x_ref, nw_ref, win_ref, wo_ref, fw_ref, o_ref, *, final_norm):
    x = x_ref[...]
    hb = _rms(x, nw_ref[...]).astype(BF16)
    hidden = wo_ref.shape[0]
    g = _dot(hb, win_ref[:, :hidden])
    u = _dot(hb, win_ref[:, hidden:])
    acc = x + _dot((_silu(g) * u).astype(BF16), wo_ref[...])
    if final_norm:
        acc = _rms(acc, fw_ref[...])
    o_ref[...] = acc


def _ffn_layer(x, norm_w, w_in, w_out, final_w, *, final_norm):
    bsz, l, d = x.shape
    hidden = w_out.shape[0]
    tile = min(FFN_ROW_TILE, l)
    kern = functools.partial(_ffn_kernel, final_norm=final_norm)
    return pl.pallas_call(
        kern,
        grid=(bsz, l // tile),
        in_specs=[_row_spec(tile, d), _const_spec((1, d)), _const_spec((d, 2 * hidden)),
                  _const_spec((hidden, d)), _const_spec((1, d))],
        out_specs=_row_spec(tile, d),
        out_shape=jax.ShapeDtypeStruct((bsz, l, d), F32),
        compiler_params=_params(("parallel", "parallel")),
        name="swiglu_ffn",
    )(x, norm_w.reshape(1, d), w_in.astype(BF16), w_out.astype(BF16), final_w.reshape(1, d))


def _proj_conv_silu(hb, w_ref, w_col0, cw_ref, cb_ref, pad_ref):
    rows = hb.shape[0]
    parts = []
    for j in range(pad_ref.shape[0]):
        c0 = j * CONV_BLOCK
        xb = _dot(hb, w_ref[:, w_col0 + c0:w_col0 + c0 + CONV_BLOCK])
        xpad = jnp.concatenate([pad_ref[j], xb], axis=0)
        conv = cw_ref[CONV_K - 1:CONV_K, c0:c0 + CONV_BLOCK] * xb
        if cb_ref is not None:
            conv = conv + cb_ref[:, c0:c0 + CONV_BLOCK]
        for k in range(CONV_K - 1):
            s = CONV_K - 1 - k
            conv = conv + cw_ref[k:k + 1, c0:c0 + CONV_BLOCK] * pltpu.roll(xpad, s, 0)[CONV_PAD:, :]
        pad_ref[j] = xb[rows - CONV_PAD:rows, :]
        parts.append(_silu(conv))
    return jnp.concatenate(parts, axis=1)


def _pair_cols(mat, h0, lane_lt64):
    return jnp.where(lane_lt64, mat[:, h0:h0 + 1], mat[:, h0 + 1:h0 + 2])


def _mamba_kernel(x_ref, nw_ref, win_ref, cw_ref, cb_ref, dtb_ref, alog_ref,
                  dsk_ref, gnw_ref, wo_ref, o_ref, pad_ref, st_ref):
    q = x_ref.shape[0]
    d_inner = wo_ref.shape[0]
    conv_ch = cw_ref.shape[1]
    n = MAMBA_D_STATE
    gs = d_inner // MAMBA_GROUPS
    heads_per_group = gs // MAMBA_HEADDIM

    @pl.when(pl.program_id(1) == 0)
    def _():
        pad_ref[...] = jnp.zeros(pad_ref.shape, F32)
        st_ref[...] = jnp.zeros(st_ref.shape, F32)

    x = x_ref[...]
    hb = _rms(x, nw_ref[...]).astype(BF16)
    dtr = _dot(hb, win_ref[:, d_inner + conv_ch:])
    xbc = _proj_conv_silu(hb, win_ref, d_inner, cw_ref, cb_ref, pad_ref)
    xs = xbc[:, :d_inner]
    bm = xbc[:, d_inner:d_inner + MAMBA_GROUPS * n]
    cm = xbc[:, d_inner + MAMBA_GROUPS * n:]

    dt = _softplus(dtr + dtb_ref[...])
    da = dt * (-jnp.exp(alog_ref[...]))
    row = lax.broadcasted_iota(jnp.int32, (q, q), 0)
    col = lax.broadcasted_iota(jnp.int32, (q, q), 1)
    causal = row >= col
    acs = _dot(causal.astype(F32), da, precision=HIGHEST)
    acs_t = acs.T
    dt_t = dt.T
    eacs = jnp.exp(acs)
    alast = acs[q - 1:q, :]
    toend = jnp.exp(alast - acs) * dt
    elast = jnp.exp(alast)
    lane_lt64 = lax.broadcasted_iota(jnp.int32, (1, LANES), 1) < MAMBA_HEADDIM

    xsb = xs.astype(BF16)
    bmb = bm.astype(BF16)
    cmb = cm.astype(BF16)
    zero_pair = jnp.zeros((q, LANES), BF16)
    z = _dot(hb, win_ref[:, :d_inner])
    y_groups = []
    for g in range(MAMBA_GROUPS):
        bg = bmb[:, g * n:(g + 1) * n]
        cg = cmb[:, g * n:(g + 1) * n]
        cb = jnp.where(causal, _dot_nt(cg, bg), 0.0)
        state = st_ref[g]
        y_off = _dot(cg, state.astype(BF16))
        y_pairs, e_pairs, w_pairs, l_pairs = [], [], [], []
        for pr in range(heads_per_group // 2):
            h0 = g * heads_per_group + 2 * pr
            lane0 = h0 * MAMBA_HEADDIM
            x_pair = xsb[:, lane0:lane0 + LANES]
            lms = []
            for h in (h0, h0 + 1):
                blocks = []
                for r0 in range(0, q, LANES):
                    r1 = r0 + LANES
                    seg = acs[r0:r1, h:h + 1] - acs_t[h:h + 1, :r1]
                    lm = (cb[r0:r1, :r1] * jnp.exp(jnp.minimum(seg, 0.0)) * dt_t[h:h + 1, :r1]).astype(BF16)
                    if r1 < q:
                        lm = jnp.concatenate([lm, jnp.zeros((LANES, q - r1), BF16)], axis=1)
                    blocks.append(lm)
                lms.append(jnp.concatenate(blocks, axis=0))
            lhs = jnp.concatenate(lms, axis=1)
            rhs = jnp.concatenate([jnp.where(lane_lt64, x_pair, zero_pair),
                                   jnp.where(lane_lt64, zero_pair, x_pair)], axis=0)
            y_pairs.append(_dot(lhs, rhs))
            e_pairs.append(_pair_cols(eacs, h0, lane_lt64))
            w_pairs.append(_pair_cols(toend, h0, lane_lt64))
            l_pairs.append(_pair_cols(elast, h0, lane_lt64))
        y_diag = jnp.concatenate(y_pairs, axis=1)
        e_exp = jnp.concatenate(e_pairs, axis=1)
        w_exp = jnp.concatenate(w_pairs, axis=1)
        l_exp = jnp.concatenate(l_pairs, axis=1)
        xs_g = xs[:, g * gs:(g + 1) * gs]
        y_groups.append(y_diag + y_off * e_exp + dsk_ref[:, g * gs:(g + 1) * gs] * xs_g)
        bg_t = bm[:, g * n:(g + 1) * n].T.astype(BF16)
        st_ref[g] = state * l_exp + _dot(bg_t, (xs_g * w_exp).astype(BF16))

    outs = []
    for g in range(MAMBA_GROUPS):
        yg = y_groups[g] * _silu(z[:, g * gs:(g + 1) * gs])
        yg = yg * lax.rsqrt(jnp.mean(yg * yg, axis=-1, keepdims=True) + NORM_EPS)
        outs.append((yg * gnw_ref[:, g * gs:(g + 1) * gs]).astype(BF16))
    y = jnp.concatenate(outs, axis=1)
    o_ref[...] = x + _dot(y, wo_ref[...])


def _mamba_layer(x, norm_w, w_in, conv_w, conv_b, dt_bias, a_log, d_skip, gn_w, w_out):
    bsz, l, d = x.shape
    d_inner = w_out.shape[0]
    heads = d_inner // MAMBA_HEADDIM
    conv_ch = d_inner + 2 * MAMBA_GROUPS * MAMBA_D_STATE
    gs = d_inner // MAMBA_GROUPS
    q = math.gcd(l, MAMBA_CHUNK)
    pad_h = LANES - heads
    win = _pad_lanes(w_in).astype(BF16)
    dtb = jnp.pad(dt_bias, (0, pad_h)).reshape(1, LANES)
    alog = jnp.pad(a_log, (0, pad_h)).reshape(1, LANES)
    dsk = jnp.repeat(d_skip, MAMBA_HEADDIM).reshape(1, d_inner)
    return pl.pallas_call(
        _mamba_kernel,
        grid=(bsz, l // q),
        in_specs=[_row_spec(q, d), _const_spec((1, d)), _const_spec(win.shape), _const_spec((CONV_K, conv_ch)),
                  _const_spec((1, conv_ch)), _const_spec((1, LANES)), _const_spec((1, LANES)),
                  _const_spec((1, d_inner)), _const_spec((1, d_inner)), _const_spec((d_inner, d))],
        out_specs=_row_spec(q, d),
        out_shape=jax.ShapeDtypeStruct((bsz, l, d), F32),
        scratch_shapes=[pltpu.VMEM((conv_ch // CONV_BLOCK, CONV_PAD, CONV_BLOCK), F32),
                        pltpu.VMEM((MAMBA_GROUPS, MAMBA_D_STATE, gs), F32)],
        compiler_params=_params(("parallel", "arbitrary")),
        name="mamba2_mixer",
    )(x, norm_w.reshape(1, d), win, conv_w, conv_b.reshape(1, conv_ch), dtb, alog, dsk,
      gn_w.reshape(1, d_inner), w_out.astype(BF16))


def _s5_discretize_kernel(lr_ref, li_ref, logdt_ref, bre_ref, bim_ref, lbr_ref, lbi_ref, bbr_ref, bbi_ref):
    lr = lr_ref[...]
    li = li_ref[...]
    dt = jnp.exp(logdt_ref[...])
    mag = jnp.exp(lr * dt)
    lbr = mag * jnp.cos(li * dt)
    lbi = mag * jnp.sin(li * dt)
    den = lr * lr + li * li
    zr = ((lbr - 1.0) * lr + lbi * li) / den
    zi = (lbi * lr - (lbr - 1.0) * li) / den
    lbr_ref[...] = lbr
    lbi_ref[...] = lbi
    bbr_ref[...] = zr * bre_ref[...] - zi * bim_ref[...]
    bbi_ref[...] = zr * bim_ref[...] + zi * bre_ref[...]


def _s5_kernel(x_ref, nw_ref, bblk_ref, cblk_ref, lre_ref, lim_ref, dsk_ref, wglu_ref, bglu_ref, o_ref,
               hn_ref, y_ref, st_ref):
    bsz, tt, d = x_ref.shape
    half = st_ref.shape[2] // 2
    n_blocks = bblk_ref.shape[0]
    in_lanes = bblk_ref.shape[1]

    @pl.when(pl.program_id(0) == 0)
    def _():
        st_ref[...] = jnp.zeros(st_ref.shape, F32)

    for b in range(bsz):
        hn_b = _rms(x_ref[b], nw_ref[...])
        for j in range(n_blocks):
            hn_ref[j, pl.ds(b, tt, stride=bsz), :] = hn_b[:, j * in_lanes:(j + 1) * in_lanes]
    for j in range(n_blocks):
        bu = _dot(hn_ref[j].astype(BF16), bblk_ref[j])
        ar = jnp.broadcast_to(lre_ref[j], (bsz, half))
        ai = jnp.broadcast_to(lim_ref[j], (bsz, half))
        st = st_ref[j]
        sr, si = st[:, :half], st[:, half:]
        states = []
        for t in range(tt):
            v = bu[t * bsz:(t + 1) * bsz]
            sr, si = ar * sr - ai * si + v[:, :half], ar * si + ai * sr + v[:, half:]
            states.append(jnp.concatenate([sr, si], axis=1).astype(BF16))
        st_ref[j] = jnp.concatenate([sr, si], axis=1)
        y_ref[j] = (_dot(jnp.concatenate(states, axis=0), cblk_ref[j])
                    + dsk_ref[:, j * in_lanes:(j + 1) * in_lanes] * hn_ref[j])
    y = jnp.concatenate([y_ref[j] for j in range(n_blocks)], axis=1)
    gb = jax.nn.gelu(y).astype(BF16)
    gl = _dot(gb, wglu_ref[...]) + bglu_ref[...]
    res = gl[:, :d] * jax.nn.sigmoid(gl[:, d:])
    for j in range(n_blocks):
        y_ref[j] = res[:, j * in_lanes:(j + 1) * in_lanes]
    for b in range(bsz):
        mix_b = jnp.concatenate([y_ref[j, pl.ds(b, tt, stride=bsz), :] for j in range(n_blocks)], axis=1)
        o_ref[b] = x_ref[b] + mix_b


def _s5_layer(x, norm_w, lam_re, lam_im, log_dt, b_re, b_im, c_re, c_im, d_skip, w_glu, b_glu):
    bsz, l, d = x.shape
    groups, state = lam_re.shape
    gsz = b_re.shape[2]
    nb = S5_LANE_BLOCKS
    gpb = groups // nb
    tt = math.gcd(l, S5_TIME_TILE)

    def expand(a):
        return jnp.repeat(a, gsz, axis=1)

    flat = (groups, state * gsz)
    lbr_e, lbi_e, bbr, bbi = pl.pallas_call(
        _s5_discretize_kernel,
        out_shape=[jax.ShapeDtypeStruct(flat, F32)] * 4,
        name="s5_discretize",
    )(expand(lam_re), expand(lam_im), jnp.broadcast_to(log_dt[:, None], flat),
      b_re.reshape(flat), b_im.reshape(flat))
    lbar_re = lbr_e.reshape(groups, state, gsz)[:, :, 0]
    lbar_im = lbi_e.reshape(groups, state, gsz)[:, :, 0]
    eye = jnp.eye(gpb, dtype=F32)

    def in_block(bb):
        bb = bb.reshape(nb, gpb, state, gsz)
        return jnp.einsum('jgpi,gh->jgihp', bb, eye).reshape(nb, gpb * gsz, gpb * state)

    def out_block(c):
        c = c.reshape(nb, gpb, gsz, state)
        return jnp.einsum('jgip,gh->jhpgi', c, eye).reshape(nb, gpb * state, gpb * gsz)

    bblk = jnp.concatenate([in_block(bbr), in_block(bbi)], axis=2).astype(BF16)
    cblk = jnp.concatenate([out_block(c_re), out_block(-c_im)], axis=1).astype(BF16)
    half = gpb * state
    lre = lbar_re.reshape(nb, 1, half)
    lim = lbar_im.reshape(nb, 1, half)
    rows = tt * bsz
    in_lanes = gpb * gsz
    return pl.pallas_call(
        _s5_kernel,
        grid=(l // tt,),
        in_specs=[pl.BlockSpec((bsz, tt, d), lambda c: (0, c, 0)), _const_spec((1, d)),
                  _const_spec((nb, in_lanes, 2 * half)), _const_spec((nb, 2 * half, in_lanes)),
                  _const_spec((nb, 1, half)), _const_spec((nb, 1, half)), _const_spec((1, d)),
                  _const_spec((d, 2 * d)), _const_spec((1, 2 * d))],
        out_specs=pl.BlockSpec((bsz, tt, d), lambda c: (0, c, 0)),
        out_shape=jax.ShapeDtypeStruct((bsz, l, d), F32),
        scratch_shapes=[pltpu.VMEM((nb, rows, in_lanes), F32), pltpu.VMEM((nb, rows, in_lanes), F32),
                        pltpu.VMEM((nb, bsz, 2 * half), F32)],
        compiler_params=_params(("arbitrary",)),
        name="s5_mixer",
    )(x, norm_w.reshape(1, d), bblk, cblk, lre, lim, d_skip.reshape(1, d),
      w_glu.astype(BF16), b_glu.reshape(1, 2 * d))


def _unit_lower_inverse(ms, cs):
    nsys, n, _ = ms.shape
    blk = GDN_INV_BLOCK
    shift = blk.bit_length() - 1
    r = lax.broadcasted_iota(jnp.int32, (n, n), 0)
    c = lax.broadcasted_iota(jnp.int32, (n, n), 1)
    same_block = lax.shift_right_logical(r, shift) == lax.shift_right_logical(c, shift)
    md = jnp.where(same_block, ms, 0.0)
    mn = jnp.where(same_block, 0.0, ms)
    tiles = n // 8
    tiles_per_block = blk // 8
    m_tiles = [md[:, 8 * k:8 * k + 8, :] for k in range(tiles)]
    rows = lax.broadcasted_iota(jnp.int32, (nsys, 8, n), 1)
    cols = lax.broadcasted_iota(jnp.int32, (nsys, 8, n), 2)
    x_tiles = [(rows + 8 * k == cols).astype(F32) for k in range(tiles)]
    for j in range(blk - 1):
        for k in range(tiles):
            b, kk = divmod(k, tiles_per_block)
            if 8 * kk + 7 <= j:
                continue
            src = b * blk + j
            row_j = x_tiles[src // 8][:, src % 8:src % 8 + 1, :]
            x_tiles[k] = x_tiles[k] - m_tiles[k][:, :, src:src + 1] * row_j
    dinv = jnp.concatenate(x_tiles, axis=1)
    systems = range(nsys)
    ps = [_dot(dinv[i].astype(BF16), mn[i].astype(BF16)).astype(BF16) for i in systems]
    ts = [dinv[i] for i in systems]
    for _ in range(cs // blk - 1):
        ts = [dinv[i] - _dot(ps[i], ts[i].astype(BF16)) for i in systems]
    return ts


def _l2norm(t):
    return t * lax.rsqrt(jnp.sum(t * t, axis=-1, keepdims=True) + 1e-6)


def _gdn_kernel(x_ref, nw_ref, win_ref, cw_ref, alog_ref, dtb_ref, gnw_ref, wo_ref,
                o_ref, pad_ref, st_ref):
    tt = x_ref.shape[0]
    heads, dk, dv = st_ref.shape
    cs = GDN_CHUNK
    nsub = tt // cs
    kd = heads * dk
    conv_ch = cw_ref.shape[1]
    gate_end = conv_ch + heads * dv

    @pl.when(pl.program_id(1) == 0)
    def _():
        pad_ref[...] = jnp.zeros(pad_ref.shape, F32)
        st_ref[...] = jnp.zeros(st_ref.shape, F32)

    x = x_ref[...]
    hb = _rms(x, nw_ref[...]).astype(BF16)
    ba = _dot(hb, win_ref[:, gate_end:])
    qkv = _proj_conv_silu(hb, win_ref, 0, cw_ref, None, pad_ref)

    beta = jax.nn.sigmoid(ba)
    gdec = -jnp.exp(alog_ref[...]) * _softplus(ba + dtb_ref[...])
    shift = cs.bit_length() - 1

    def chunk_masks(size):
        row = lax.broadcasted_iota(jnp.int32, (size, size), 0)
        col = lax.broadcasted_iota(jnp.int32, (size, size), 1)
        same_chunk = lax.shift_right_logical(row, shift) == lax.shift_right_logical(col, shift)
        return jnp.logical_and(row >= col, same_chunk), jnp.logical_and(row > col, same_chunk)

    gc = _dot(chunk_masks(tt)[0].astype(F32), gdec, precision=HIGHEST)
    egc = jnp.exp(gc)
    gc_t = gc.T
    sb = min(tt, LANES)
    nsb = tt // sb
    chunk_causal, chunk_strict = chunk_masks(sb)
    last_rows = [gc[(s + 1) * cs - 1:(s + 1) * cs, :] for s in range(nsub)]
    to_end = jnp.exp(jnp.concatenate([last_rows[s] - gc[s * cs:(s + 1) * cs, :] for s in range(nsub)], axis=0))
    e_last = [jnp.exp(last_rows[s]) for s in range(nsub)]
    hs = range(heads)

    q_all, k_all, kb_all, rhs_all, qe_all, kdec_all = [], [], [], [], [], []
    for h in hs:
        lane = heads + h
        qh = _l2norm(qkv[:, h * dk:(h + 1) * dk]) * (dk ** -0.5)
        kh = _l2norm(qkv[:, kd + h * dk:kd + (h + 1) * dk])
        vh = qkv[:, 2 * kd + h * dv:2 * kd + (h + 1) * dv]
        bcol = beta[:, h:h + 1]
        e_col = egc[:, lane:lane + 1]
        kbh = kh * bcol
        q_all.append(qh)
        k_all.append(kh)
        kb_all.append(kbh)
        rhs_all.append(jnp.concatenate([vh * bcol, kbh * e_col], axis=1).astype(BF16))
        qe_all.append(qh * e_col)
        kdec_all.append(kh * to_end[:, lane:lane + 1])
    hsb = [(h, a) for h in hs for a in range(nsb)]
    g_all = {(h, a): _dot_nt(jnp.concatenate([kb_all[h][a * sb:(a + 1) * sb], q_all[h][a * sb:(a + 1) * sb]],
                                             axis=0).astype(BF16), k_all[h][a * sb:(a + 1) * sb].astype(BF16))
             for h, a in hsb}
    ms, attn_all = [], {}
    for h, a in hsb:
        lane = heads + h
        seg = gc[a * sb:(a + 1) * sb, lane:lane + 1] - gc_t[lane:lane + 1, a * sb:(a + 1) * sb]
        decay = jnp.exp(jnp.where(chunk_causal, seg, -jnp.inf))
        ms.append(jnp.where(chunk_strict, g_all[h, a][:sb] * decay, 0.0))
        attn_all[h, a] = (g_all[h, a][sb:] * decay).astype(BF16)
    gate = _dot(hb, win_ref[:, conv_ch:gate_end])
    t_all = _unit_lower_inverse(jnp.stack(ms), cs)
    sol_parts = {(h, a): _dot(t_all[h * nsb + a].astype(BF16), rhs_all[h][a * sb:(a + 1) * sb]) for h, a in hsb}
    sol_all = [jnp.concatenate([sol_parts[h, a] for a in range(nsb)], axis=0) for h in hs]

    states = [st_ref[h] for h in hs]
    v_new_all = [[] for _ in hs]
    o_off_all = [[] for _ in hs]
    for s in range(nsub):
        sl = slice(s * cs, (s + 1) * cs)
        ws_all = [_dot(jnp.concatenate([sol_all[h][sl, dv:], qe_all[h][sl]], axis=0).astype(BF16),
                       states[h].astype(BF16)) for h in hs]
        k_dec_t = [kdec_all[h][sl].T.astype(BF16) for h in hs]
        for h in hs:
            v_new_all[h].append((sol_all[h][sl, :dv] - ws_all[h][:cs]).astype(BF16))
            o_off_all[h].append(ws_all[h][cs:])
        states = [states[h] * e_last[s][:, heads + h:heads + h + 1] + _dot(k_dec_t[h], v_new_all[h][s])
                  for h in hs]
    for h in hs:
        st_ref[h] = states[h]
    v_new_cat = [jnp.concatenate(v_new_all[h], axis=0) for h in hs]
    o_in_parts = {(h, a): _dot(attn_all[h, a], v_new_cat[h][a * sb:(a + 1) * sb]) for h, a in hsb}
    o_in_all = [jnp.concatenate([o_in_parts[h, a] for a in range(nsb)], axis=0) for h in hs]
    o_heads = []
    for h in hs:
        oh = jnp.concatenate(o_off_all[h], axis=0) + o_in_all[h]
        oh = oh * lax.rsqrt(jnp.mean(oh * oh, axis=-1, keepdims=True) + NORM_EPS)
        oh = oh * gnw_ref[...] * _silu(gate[:, h * dv:(h + 1) * dv])
        o_heads.append(oh.astype(BF16))
    o_ref[...] = x + _dot(jnp.concatenate(o_heads, axis=1), wo_ref[...])


def _gdn_layer(x, norm_w, w_in, conv_w, a_log, dt_bias, gn_w, w_out):
    bsz, l, d = x.shape
    dk, dv = GDN_DK, GDN_DV
    heads = w_out.shape[0] // dv
    conv_ch = heads * (2 * dk + dv)
    tt = math.gcd(l, GDN_TIME_TILE)
    lane_pad = LANES - 2 * heads
    win = _pad_lanes(w_in).astype(BF16)
    alog = jnp.pad(a_log, (heads, lane_pad)).reshape(1, LANES)
    dtb = jnp.pad(dt_bias, (heads, lane_pad)).reshape(1, LANES)
    return pl.pallas_call(
        _gdn_kernel,
        grid=(bsz, l // tt),
        in_specs=[_row_spec(tt, d), _const_spec((1, d)), _const_spec(win.shape), _const_spec((CONV_K, conv_ch)),
                  _const_spec((1, LANES)), _const_spec((1, LANES)), _const_spec((1, dv)),
                  _const_spec((heads * dv, d))],
        out_specs=_row_spec(tt, d),
        out_shape=jax.ShapeDtypeStruct((bsz, l, d), F32),
        scratch_shapes=[pltpu.VMEM((conv_ch // CONV_BLOCK, CONV_PAD, CONV_BLOCK), F32),
                        pltpu.VMEM((heads, dk, dv), F32)],
        compiler_params=_params(("parallel", "arbitrary")),
        name="gdn_mixer",
    )(x, norm_w.reshape(1, d), win, conv_w, alog, dtb, gn_w.reshape(1, dv), w_out.astype(BF16))


def kernel(x, mix_norm_w, mamba_w_in, mamba_conv_w, mamba_conv_b, mamba_dt_bias, mamba_a_log, mamba_d, mamba_norm_w, mamba_w_out, s5_lam_re, s5_lam_im, s5_log_dt, s5_b_re, s5_b_im, s5_c_re, s5_c_im, s5_d, s5_w_glu, s5_b_glu, gdn_w_in, gdn_conv_w, gdn_a_log, gdn_dt_bias, gdn_norm_w, gdn_w_out, ffn_norm_w, ffn_w_in, ffn_w_out, final_norm_w):
    depth = mix_norm_w.shape[0]
    h = x
    for i in range(depth):
        kind, j = i % 3, i // 3
        if kind == 0:
            h = _mamba_layer(h, mix_norm_w[i], mamba_w_in[j], mamba_conv_w[j], mamba_conv_b[j], mamba_dt_bias[j],
                             mamba_a_log[j], mamba_d[j], mamba_norm_w[j], mamba_w_out[j])
        elif kind == 1:
            h = _s5_layer(h, mix_norm_w[i], s5_lam_re[j], s5_lam_im[j], s5_log_dt[j], s5_b_re[j], s5_b_im[j],
                          s5_c_re[j], s5_c_im[j], s5_d[j], s5_w_glu[j], s5_b_glu[j])
        else:
            h = _gdn_layer(h, mix_norm_w[i], gdn_w_in[j], gdn_conv_w[j], gdn_a_log[j], gdn_dt_bias[j],
                           gdn_norm_w[j], gdn_w_out[j])
        h = _ffn_layer(h, ffn_norm_w[i], ffn_w_in[i], ffn_w_out[i], final_norm_w, final_norm=i == depth - 1)
    return h
```
